```python
import jax, jax.numpy as jnp
from jax import lax
import numpy as np

D_MODEL = 2048
BATCH = 4
SEQ = 4096
DEPTH = 1

CHUNK = 64
LEFT_CHUNKS = 8
BAND = LEFT_CHUNKS + 1
ATT_HEADS = 8
ATT_HEAD_DIM = 128
ATT_WIDTH = ATT_HEADS * ATT_HEAD_DIM
REL_CLIP = 128
CONV_WIDTH = 1024
CONV_K = 3
MEM_LEN = 256
MEM_HEADS = 4
MEM_HEAD_DIM = D_MODEL // MEM_HEADS
PEER_HEADS = 8
PEER_NKEYS = 128
PEER_EXPERTS = PEER_NKEYS * PEER_NKEYS
PEER_QDIM = 256
PEER_HALF = PEER_QDIM // 2
PEER_TOPK = 16
PEER_TOKEN_BLOCK = 128
N_BRANCHES = 2
EPS = 1e-6
NEG_INF = -1e30
IN_SIZES = (ATT_WIDTH, ATT_WIDTH, ATT_WIDTH, CONV_WIDTH, CONV_WIDTH, CONV_WIDTH, D_MODEL, D_MODEL)
IN_COLS = sum(IN_SIZES)

kernel_name = 'hybrid_chunked_attn_shortconv_peer_block'


def rmsnorm(x, g):
    xf = x.astype(jnp.float32)
    y = xf * lax.rsqrt(jnp.mean(xf * xf, axis=-1, keepdims=True) + EPS)
    return (y * g.astype(jnp.float32)).astype(x.dtype)


def band_rel_bias(rel_table):
    a = jnp.arange(CHUNK)[:, None, None]
    j = jnp.arange(BAND)[None, :, None]
    b = jnp.arange(CHUNK)[None, None, :]
    rel = (LEFT_CHUNKS - j) * CHUNK + a - b
    idx = jnp.clip(rel, -REL_CLIP, REL_CLIP) + REL_CLIP
    return rel_table[:, idx]


def chunked_band_attention(q, k, v, rel_table):
    bsz, seq = q.shape[0], q.shape[1]
    nc = seq // CHUNK
    shp = (bsz, nc, CHUNK, ATT_HEADS, ATT_HEAD_DIM)
    qc = q.reshape(shp)
    pad = ((0, 0), (LEFT_CHUNKS, 0), (0, 0), (0, 0), (0, 0))
    kp = jnp.pad(k.reshape(shp), pad)
    vp = jnp.pad(v.reshape(shp), pad)
    s = jnp.stack([jnp.einsum('bnqhd,bnkhd->bnhqk', qc, kp[:, j:j + nc]) for j in range(BAND)], axis=4)
    s = s.astype(jnp.float32) * (ATT_HEAD_DIM ** -0.5)
    s = s + band_rel_bias(rel_table).astype(jnp.float32)
    src = jnp.arange(nc)[:, None] + jnp.arange(BAND)[None, :] - LEFT_CHUNKS
    valid = (src >= 0)[None, :, None, None, :, None]
    s = jnp.where(valid, s, NEG_INF)
    p = jax.nn.softmax(s.reshape(bsz, nc, ATT_HEADS, CHUNK, BAND * CHUNK), axis=-1)
    p = p.reshape(bsz, nc, ATT_HEADS, CHUNK, BAND, CHUNK).astype(v.dtype)
    o = jnp.einsum('bnhqk,bnkhd->bnqhd', p[:, :, :, :, 0], vp[:, 0:nc])
    for j in range(1, BAND):
        o = o + jnp.einsum('bnhqk,bnkhd->bnqhd', p[:, :, :, :, j], vp[:, j:j + nc])
    return o.reshape(bsz, seq, ATT_WIDTH)


def causal_depthwise_conv(u, w, b):
    seq = u.shape[1]
    up = jnp.pad(u, ((0, 0), (CONV_K - 1, 0), (0, 0)))
    y = up[:, 0:seq] * w[0] + b
    for i in range(1, CONV_K):
        y = y + up[:, i:i + seq] * w[i]
    return y


def memory_cross_attention(h, mem_n, w_cq, w_ck, w_cv, w_co):
    bsz, seq = h.shape[0], h.shape[1]
    mlen = mem_n.shape[1]
    q = (h @ w_cq).reshape(bsz, seq, MEM_HEADS, MEM_HEAD_DIM)
    k = (mem_n @ w_ck).reshape(bsz, mlen, MEM_HEADS, MEM_HEAD_DIM)
    v = (mem_n @ w_cv).reshape(bsz, mlen, MEM_HEADS, MEM_HEAD_DIM)
    s = jnp.einsum('bshd,bmhd->bhsm', q, k).astype(jnp.float32) * (MEM_HEAD_DIM ** -0.5)
    p = jax.nn.softmax(s, axis=-1).astype(v.dtype)
    o = jnp.einsum('bhsm,bmhd->bshd', p, v).reshape(bsz, seq, D_MODEL)
    return o @ w_co


def peer_ffn(h, w_pq, sub_keys, expert_u, expert_v):
    bsz, seq, dm = h.shape
    q = (h @ w_pq).reshape(bsz, seq, PEER_HEADS, 2, PEER_HALF)
    s = jnp.einsum('bshpd,hpkd->bshpk', q, sub_keys).astype(jnp.float32)
    s1, i1 = lax.top_k(s[:, :, :, 0], PEER_TOPK)
    s2, i2 = lax.top_k(s[:, :, :, 1], PEER_TOPK)
    cand = (s1[..., :, None] + s2[..., None, :]).reshape(bsz, seq, PEER_HEADS, PEER_TOPK * PEER_TOPK)
    cand_idx = (i1[..., :, None] * PEER_NKEYS + i2[..., None, :]).reshape(bsz, seq, PEER_HEADS, PEER_TOPK * PEER_TOPK)
    top_s, pos = lax.top_k(cand, PEER_TOPK)
    idx = jnp.take_along_axis(cand_idx, pos, axis=-1)
    gates = jax.nn.softmax(top_s, axis=-1).astype(h.dtype)
    n_tok = bsz * seq
    n_sel = PEER_HEADS * PEER_TOPK
    nb = n_tok // PEER_TOKEN_BLOCK
    hb = h.reshape(nb, PEER_TOKEN_BLOCK, dm)
    ib = idx.reshape(nb, PEER_TOKEN_BLOCK, n_sel)
    gb = gates.reshape(nb, PEER_TOKEN_BLOCK, n_sel)

    def block(args):
        hx, ix, gx = args
        u_sel = expert_u[ix]
        act = jax.nn.gelu(jnp.einsum('td,ted->te', hx, u_sel), approximate=False) * gx
        v_sel = expert_v[ix]
        return jnp.einsum('te,ted->td', act, v_sel)

    out = lax.map(block, (hb, ib, gb))
    return out.reshape(bsz, seq, dm)


def setup_inputs(seed: int = 0) -> dict:
    key = jax.random.key(seed)
    ks = jax.random.split(key, 24)

    def nrm(k, shape, scale):
        return jax.random.normal(k, shape, jnp.float32) * scale

    def gain(k, shape):
        return 1.0 + 0.05 * jax.random.normal(k, shape, jnp.float32)

    L, D = DEPTH, D_MODEL
    return {
        'x': nrm(ks[0], (BATCH, SEQ, D), 1.0),
        'mem': nrm(ks[1], (BATCH, MEM_LEN, D), 1.0),
        'norm_mix': gain(ks[2], (L, D)),
        'w_in': nrm(ks[3], (L, D, IN_COLS), D ** -0.5),
        'conv_w': nrm(ks[4], (L, CONV_K, CONV_WIDTH), CONV_K ** -0.5),
        'conv_b': nrm(ks[5], (L, CONV_WIDTH), 0.02),
        'rel_bias': nrm(ks[6], (L, ATT_HEADS, 2 * REL_CLIP + 1), 0.5),
        'w_att_out': nrm(ks[7], (L, ATT_WIDTH, D), ATT_WIDTH ** -0.5),
        'w_conv_out': nrm(ks[8], (L, CONV_WIDTH, D), CONV_WIDTH ** -0.5),
        'w_mix_out': nrm(ks[9], (L, D, D), D ** -0.5),
        'norm_cross': gain(ks[10], (L, D)),
        'norm_mem': gain(ks[11], (L, D)),
        'w_cq': nrm(ks[12], (L, D, D), D ** -0.5),
        'w_ck': nrm(ks[13], (L, D, D), D ** -0.5),
        'w_cv': nrm(ks[14], (L, D, D), D ** -0.5),
        'w_co': nrm(ks[15], (L, D, D), D ** -0.5),
        'norm_peer': gain(ks[16], (L, D)),
        'w_pq': nrm(ks[17], (L, D, PEER_HEADS * PEER_QDIM), D ** -0.5),
        'sub_keys': nrm(ks[18], (L, PEER_HEADS, 2, PEER_NKEYS, PEER_HALF), PEER_HALF ** -0.5),
        'expert_u': nrm(ks[19], (L, PEER_EXPERTS, D), D ** -0.5),
        'expert_v': nrm(ks[20], (L, PEER_EXPERTS, D), PEER_HEADS ** -0.5),
        'norm_final': gain(ks[21], (D,)),
    }


def reference(x, mem, norm_mix, w_in, conv_w, conv_b, rel_bias, w_att_out, w_conv_out, w_mix_out,
              norm_cross, norm_mem, w_cq, w_ck, w_cv, w_co, norm_peer, w_pq, sub_keys,
              expert_u, expert_v, norm_final):
    bsz, seq = x.shape[0], x.shape[1]
    split_at = [int(c) for c in np.cumsum(IN_SIZES)[:-1]]
    for l in range(DEPTH):
        h = rmsnorm(x, norm_mix[l])
        proj = h @ w_in[l]
        q, k, v, u, bgate, cgate, ga, gb = jnp.split(proj, split_at, axis=-1)
        hs = (bsz, seq, ATT_HEADS, ATT_HEAD_DIM)
        y_att = chunked_band_attention(q.reshape(hs), k.reshape(hs), v.reshape(hs), rel_bias[l]) @ w_att_out[l]
        y_conv = (bgate * causal_depthwise_conv(cgate * u, conv_w[l], conv_b[l])) @ w_conv_out[l]
        merged = jax.nn.sigmoid(ga) * y_att + jax.nn.sigmoid(gb) * y_conv
        x = x + merged @ w_mix_out[l]
        h = rmsnorm(x, norm_cross[l])
        mem_n = rmsnorm(mem, norm_mem[l])
        x = x + memory_cross_attention(h, mem_n, w_cq[l], w_ck[l], w_cv[l], w_co[l])
        h = rmsnorm(x, norm_peer[l])
        x = x + peer_ffn(h, w_pq[l], sub_keys[l], expert_u[l], expert_v[l])
    return rmsnorm(x, norm_final)
```

```python
import functools
import math

import jax
import jax.numpy as jnp
from jax import lax
from jax.experimental import pallas as pl
from jax.experimental.pallas import tpu as pltpu

F32 = jnp.float32
BF16 = jnp.bfloat16

CHUNK = 64
LEFT_CHUNKS = 8
ATT_HEADS = 8
ATT_HEAD_DIM = 128
ATT_WIDTH = ATT_HEADS * ATT_HEAD_DIM
REL_CLIP = 128
CONV_WIDTH = 1024
MEM_HEADS = 4
PEER_HEADS = 8
PEER_NKEYS = 128
PEER_HALF = 128
PEER_TOPK = 16
EPS = 1e-6
NEG_INF = -1e30

LANES = 128
ATT_QBLOCK = 512
ATT_SUB = 256
ATT_KWIN = ATT_SUB + LEFT_CHUNKS * CHUNK
VMEM_LIMIT = 56 * 1024 * 1024


def _cparams(*sem):
    return pltpu.CompilerParams(dimension_semantics=sem, vmem_limit_bytes=VMEM_LIMIT)


def _norm_matmul_kernel(x_ref, g_ref, w_ref, o_ref, *rest, emit_h):
    if emit_h:
        hout_ref, h_ref = rest
    else:
        (h_ref,) = rest

    @pl.when(pl.program_id(1) == 0)
    def _():
        x = x_ref[...]
        ms = jnp.mean(x * x, axis=-1, keepdims=True)
        h = (x * lax.rsqrt(ms + EPS) * g_ref[...]).astype(BF16)
        h_ref[...] = h
        if emit_h:
            hout_ref[...] = h

    o_ref[...] = jnp.dot(h_ref[...], w_ref[...], preferred_element_type=F32).astype(o_ref.dtype)


def _norm_matmul(x, gain, w, out_dtype, *, tm=1024, tn=1024, emit_h=False):
    n, d = x.shape
    c = w.shape[1]
    tm = min(tm, n)
    out_shape = [jax.ShapeDtypeStruct((n, c), out_dtype)]
    out_specs = [pl.BlockSpec((tm, tn), lambda i, j: (i, j))]
    if emit_h:
        out_shape.append(jax.ShapeDtypeStruct((n, d), BF16))
        out_specs.append(pl.BlockSpec((tm, d), lambda i, j: (i, 0)))
    res = pl.pallas_call(
        functools.partial(_norm_matmul_kernel, emit_h=emit_h),
        grid=(n // tm, c // tn),
        in_specs=[
            pl.BlockSpec((tm, d), lambda i, j: (i, 0)),
            pl.BlockSpec((1, d), lambda i, j: (0, 0)),
            pl.BlockSpec((d, tn), lambda i, j: (0, j)),
        ],
        out_specs=out_specs,
        out_shape=out_shape,
        scratch_shapes=[pltpu.VMEM((tm, d), BF16)],
        compiler_params=_cparams("parallel", "arbitrary"),
        name="norm_matmul",
    )(x, gain.reshape(1, d), w)
    return res if emit_h else res[0]


def _matmul_residual_kernel(a_ref, w_ref, r_ref, o_ref):
    o_ref[...] = r_ref[...] + jnp.dot(a_ref[...], w_ref[...], preferred_element_type=F32)


def _matmul_residual(a, w, res, *, tm=1024, tn=1024):
    n, k = a.shape
    c = w.shape[1]
    return pl.pallas_call(
        _matmul_residual_kernel,
        grid=(n // tm, c // tn),
        in_specs=[
            pl.BlockSpec((tm, k), lambda i, j: (i, 0)),
            pl.BlockSpec((k, tn), lambda i, j: (0, j)),
            pl.BlockSpec((tm, tn), lambda i, j: (i, j)),
        ],
        out_specs=pl.BlockSpec((tm, tn), lambda i, j: (i, j)),
        out_shape=jax.ShapeDtypeStruct((n, c), F32),
        compiler_params=_cparams("parallel", "arbitrary"),
        name="matmul_residual",
    )(a, w, res)


def _band_attention_kernel(q_ref, kp_ref, kc_ref, vp_ref, vc_ref, bias_ref, o_ref, kwin_ref, vwin_ref):
    first = pl.program_id(1) == 0
    kwin_ref[0:ATT_QBLOCK, :] = kp_ref[...]
    kwin_ref[ATT_QBLOCK:, :] = kc_ref[...]
    vwin_ref[0:ATT_QBLOCK, :] = vp_ref[...]
    vwin_ref[ATT_QBLOCK:, :] = vc_ref[...]
    scale = ATT_HEAD_DIM ** -0.5
    col = lax.broadcasted_iota(jnp.int32, (ATT_SUB, ATT_KWIN), 1)
    for s in range(ATT_QBLOCK // ATT_SUB):
        r0 = s * ATT_SUB
        dead = jnp.logical_and(first, col < ATT_QBLOCK - r0)
        for h in range(ATT_HEADS):
            c0 = h * ATT_HEAD_DIM
            qh = q_ref[r0:r0 + ATT_SUB, c0:c0 + ATT_HEAD_DIM]
            kh = kwin_ref[r0:r0 + ATT_KWIN, c0:c0 + ATT_HEAD_DIM]
            vh = vwin_ref[r0:r0 + ATT_KWIN, c0:c0 + ATT_HEAD_DIM]
            sc = lax.dot_general(qh, kh, (((1,), (1,)), ((), ())), preferred_element_type=F32)
            sc = sc * scale + bias_ref[h]
            sc = jnp.where(dead, NEG_INF, sc)
            m = jnp.max(sc, axis=-1, keepdims=True)
            p = jnp.exp(sc - m)
            p = p / jnp.sum(p, axis=-1, keepdims=True)
            o = jnp.dot(p.astype(BF16), vh, preferred_element_type=F32)
            o_ref[r0:r0 + ATT_SUB, c0:c0 + ATT_HEAD_DIM] = o.astype(o_ref.dtype)


def _band_bias(rel_table):
    r = jnp.arange(ATT_SUB)[:, None]
    c = jnp.arange(ATT_KWIN)[None, :]
    idx = jnp.clip(LEFT_CHUNKS * CHUNK + r - c, -REL_CLIP, REL_CLIP) + REL_CLIP
    qc, kc = r // CHUNK, c // CHUNK
    in_band = jnp.logical_and(kc >= qc, kc <= qc + LEFT_CHUNKS)
    return jnp.where(in_band[None], rel_table[:, idx], NEG_INF).astype(F32)


def _band_attention(qkv, bias, bsz, seq):
    n = bsz * seq
    nb = seq // ATT_QBLOCK
    blk = (ATT_QBLOCK, ATT_WIDTH)
    cur = lambda col: (lambda b, i: (b * nb + i, col))
    prev = lambda col: (lambda b, i: (b * nb + jnp.maximum(i - 1, 0), col))
    return pl.pallas_call(
        _band_attention_kernel,
        grid=(bsz, nb),
        in_specs=[
            pl.BlockSpec(blk, cur(0)),
            pl.BlockSpec(blk, prev(1)),
            pl.BlockSpec(blk, cur(1)),
            pl.BlockSpec(blk, prev(2)),
            pl.BlockSpec(blk, cur(2)),
            pl.BlockSpec((ATT_HEADS, ATT_SUB, ATT_KWIN), lambda b, i: (0, 0, 0)),
        ],
        out_specs=pl.BlockSpec(blk, lambda b, i: (b * nb + i, 0)),
        out_shape=jax.ShapeDtypeStruct((n, ATT_WIDTH), BF16),
        scratch_shapes=[pltpu.VMEM((2 * ATT_QBLOCK, ATT_WIDTH), BF16),
                        pltpu.VMEM((2 * ATT_QBLOCK, ATT_WIDTH), BF16)],
        compiler_params=_cparams("parallel", "arbitrary"),
        name="band_attention",
    )(qkv, qkv, qkv, qkv, qkv, bias)


def _gated_merge_kernel(att_ref, u_ref, b_ref, c_ref, uh_ref, ch_ref, cw_ref, cb_ref,
                        wa_ref, wc_ref, ga_ref, gb_ref, o_ref, yc_ref, *, tiles_per_seq):
    @pl.when(pl.program_id(1) == 0)
    def _():
        tm = u_ref.shape[0]
        z = c_ref[...] * u_ref[...]
        keep = (pl.program_id(0) % tiles_per_seq != 0).astype(F32)
        zh = ch_ref[...] * uh_ref[...] * keep
        row = lax.broadcasted_iota(jnp.int32, z.shape, 0)
        z1 = jnp.where(row == 0, zh[7:8], pltpu.roll(z, 1, 0))
        z2 = jnp.where(row == 0, zh[6:7], jnp.where(row == 1, zh[7:8], pltpu.roll(z, 2, 0)))
        y = z2 * cw_ref[0:1] + cb_ref[...]
        y = y + z1 * cw_ref[1:2]
        y = y + z * cw_ref[2:3]
        yc_ref[...] = (b_ref[...] * y).astype(BF16)

    y_att = jnp.dot(att_ref[...], wa_ref[...], preferred_element_type=F32)
    y_conv = jnp.dot(yc_ref[...], wc_ref[...], preferred_element_type=F32)
    merged = jax.nn.sigmoid(ga_ref[...]) * y_att + jax.nn.sigmoid(gb_ref[...]) * y_conv
    o_ref[...] = merged.astype(o_ref.dtype)


def _gated_merge(att, ubc, gates, conv_w, conv_b, w_att_out, w_conv_out, seq, *, tm=512, tn=1024):
    n = att.shape[0]
    d = w_att_out.shape[1]
    halo = 8
    hpt = tm // halo
    col = lambda cc: (lambda i, j: (i, cc))
    halo_map = lambda cc: (lambda i, j: (jnp.maximum(i * hpt - 1, 0), cc))
    nj = d // tn
    return pl.pallas_call(
        functools.partial(_gated_merge_kernel, tiles_per_seq=seq // tm),
        grid=(n // tm, nj),
        in_specs=[
            pl.BlockSpec((tm, ATT_WIDTH), col(0)),
            pl.BlockSpec((tm, CONV_WIDTH), col(0)),
            pl.BlockSpec((tm, CONV_WIDTH), col(1)),
            pl.BlockSpec((tm, CONV_WIDTH), col(2)),
            pl.BlockSpec((halo, CONV_WIDTH), halo_map(0)),
            pl.BlockSpec((halo, CONV_WIDTH), halo_map(2)),
            pl.BlockSpec((3, CONV_WIDTH), lambda i, j: (0, 0)),
            pl.BlockSpec((1, CONV_WIDTH), lambda i, j: (0, 0)),
            pl.BlockSpec((ATT_WIDTH, tn), lambda i, j: (0, j)),
            pl.BlockSpec((CONV_WIDTH, tn), lambda i, j: (0, j)),
            pl.BlockSpec((tm, tn), lambda i, j: (i, j)),
            pl.BlockSpec((tm, tn), lambda i, j: (i, j + nj)),
        ],
        out_specs=pl.BlockSpec((tm, tn), lambda i, j: (i, j)),
        out_shape=jax.ShapeDtypeStruct((n, d), BF16),
        scratch_shapes=[pltpu.VMEM((tm, CONV_WIDTH), BF16)],
        compiler_params=_cparams("parallel", "arbitrary"),
        name="gated_merge",
    )(att, ubc, ubc, ubc, ubc, ubc, conv_w, conv_b.reshape(1, CONV_WIDTH),
      w_att_out, w_conv_out, gates, gates)


def _cross_attention_kernel(q_ref, k_ref, v_ref, o_ref):
    dh = q_ref.shape[1] // MEM_HEADS
    scale = dh ** -0.5
    for h in range(MEM_HEADS):
        c0 = h * dh
        sc = lax.dot_general(q_ref[:, c0:c0 + dh], k_ref[:, c0:c0 + dh],
                             (((1,), (1,)), ((), ())), preferred_element_type=F32) * scale
        m = jnp.max(sc, axis=-1, keepdims=True)
        p = jnp.exp(sc - m)
        p = p / jnp.sum(p, axis=-1, keepdims=True)
        o = jnp.dot(p.astype(BF16), v_ref[:, c0:c0 + dh], preferred_element_type=F32)
        o_ref[:, c0:c0 + dh] = o.astype(o_ref.dtype)


def _cross_attention(q, k, v, bsz, seq, mlen, *, tm=512):
    n, d = q.shape
    nb = seq // tm
    return pl.pallas_call(
        _cross_attention_kernel,
        grid=(bsz, nb),
        in_specs=[
            pl.BlockSpec((tm, d), lambda b, i: (b * nb + i, 0)),
            pl.BlockSpec((mlen, d), lambda b, i: (b, 0)),
            pl.BlockSpec((mlen, d), lambda b, i: (b, 0)),
        ],
        out_specs=pl.BlockSpec((tm, d), lambda b, i: (b * nb + i, 0)),
        out_shape=jax.ShapeDtypeStruct((n, d), BF16),
        compiler_params=_cparams("parallel", "arbitrary"),
        name="cross_attention",
    )(q, k, v)


def _extract_top(s, count):
    rows = lax.broadcasted_iota(jnp.int32, s.shape, 0)
    out = []
    for _ in range(count):
        m = jnp.max(s, axis=0, keepdims=True)
        out.append(m)
        first = jnp.min(jnp.where(s == m, rows, s.shape[0]), axis=0, keepdims=True)
        s = jnp.where(rows == first, -jnp.inf, s)
    return out


def _peer_select_kernel(q_ref, sk_ref, a_ref, b_ref, st_ref):
    t = q_ref.shape[0]
    for h in range(PEER_HEADS):
        c0 = h * 2 * PEER_HALF
        dn = (((1,), (1,)), ((), ()))
        a_ref[h] = lax.dot_general(sk_ref[h, 0], q_ref[:, c0:c0 + PEER_HALF], dn,
                                   preferred_element_type=F32)
        b_ref[h] = lax.dot_general(sk_ref[h, 1], q_ref[:, c0 + PEER_HALF:c0 + 2 * PEER_HALF], dn,
                                   preferred_element_type=F32)

    lane_tiles = t // LANES
    row16 = lax.broadcasted_iota(jnp.int32, (PEER_TOPK, LANES), 0)
    row8 = lax.broadcasted_iota(jnp.int32, (8, LANES), 0)

    def body(it, carry):
        h = it // lane_tiles
        l0 = pl.multiple_of((it % lane_tiles) * LANES, LANES)
        a_top = _extract_top(a_ref[h, :, pl.ds(l0, LANES)], PEER_TOPK)
        b_rows = _extract_top(b_ref[h, :, pl.ds(l0, LANES)], PEER_TOPK)
        b_top = jnp.zeros((PEER_TOPK, LANES), F32)
        for k in range(PEER_TOPK):
            b_top = jnp.where(row16 == k, b_rows[k], b_top)
        pieces = [b_top + a_top[0]]
        for k1 in range(2, PEER_TOPK + 1):
            pieces.append(jnp.where(row8 < PEER_TOPK // k1, b_top[0:8] + a_top[k1 - 1], -jnp.inf))
        cand = jnp.concatenate(pieces, axis=0)
        best = _extract_top(cand, PEER_TOPK)
        z = jnp.zeros((1, LANES), F32)
        for v in best:
            z = z + jnp.exp(v - best[0])
        thr = best[PEER_TOPK - 1]
        log_norm = a_top[0] + jnp.log(z)
        st = jnp.where(row8 == 0, thr, jnp.where(row8 == 1, log_norm, jnp.where(row8 == 2, b_rows[0], 0.0)))
        st_ref[h, :, pl.ds(l0, LANES)] = st
        return carry

    lax.fori_loop(0, PEER_HEADS * lane_tiles, body, 0)


def _peer_select(q, sub_keys, *, t=512):
    n = q.shape[0]
    nh = PEER_HEADS
    score = jax.ShapeDtypeStruct((nh, PEER_NKEYS, n), F32)
    return pl.pallas_call(
        _peer_select_kernel,
        grid=(n // t,),
        in_specs=[
            pl.BlockSpec((t, q.shape[1]), lambda i: (i, 0)),
            pl.BlockSpec(sub_keys.shape, lambda i: (0, 0, 0, 0)),
        ],
        out_specs=[
            pl.BlockSpec((nh, PEER_NKEYS, t), lambda i: (0, 0, i)),
            pl.BlockSpec((nh, PEER_NKEYS, t), lambda i: (0, 0, i)),
            pl.BlockSpec((nh, 8, t), lambda i: (0, 0, i)),
        ],
        out_shape=[score, score, jax.ShapeDtypeStruct((nh, 8, n), F32)],
        compiler_params=_cparams("parallel"),
        name="peer_select",
    )(q, sub_keys)


def _peer_experts_kernel(h_ref, a_ref, b_ref, st_ref, u_ref, v_ref, x_ref, g_ref, o_ref,
                         eb_ref, w_ref, hu_ref, act_ref):
    e = pl.program_id(1)
    t = h_ref.shape[0]
    eb = u_ref.shape[0]

    @pl.when(e == 0)
    def _():
        o_ref[...] = jnp.zeros_like(o_ref)
        for h in range(PEER_HEADS):
            eb_ref[h] = jnp.exp(b_ref[h] - st_ref[h, 2:3, :])
            w_ref[h] = jnp.exp(a_ref[h] - st_ref[h, 1:2, :])

    hu_ref[...] = lax.dot_general(u_ref[...], h_ref[...], (((1,), (1,)), ((), ())),
                                  preferred_element_type=F32)
    assert eb == 8 * PEER_NKEYS
    i0 = pl.multiple_of(e * 8, 8)
    for lt in range(t // LANES):
        ls = slice(lt * LANES, (lt + 1) * LANES)
        a_rows = [a_ref[h, pl.ds(i0, 8), ls] for h in range(PEER_HEADS)]
        w_rows = [w_ref[h, pl.ds(i0, 8), ls] for h in range(PEER_HEADS)]
        for ii in range(8):
            r0 = ii * PEER_NKEYS
            gate = jnp.zeros((PEER_NKEYS, LANES), F32)
            for h in range(PEER_HEADS):
                pair = b_ref[h, :, ls] + a_rows[h][ii:ii + 1]
                sel = pair >= st_ref[h, 0:1, ls]
                gate = gate + jnp.where(sel, eb_ref[h, :, ls], 0.0) * w_rows[h][ii:ii + 1]
            hu = hu_ref[r0:r0 + PEER_NKEYS, ls]
            act = 0.5 * hu * (1.0 + lax.erf(hu * (2.0 ** -0.5)))
            act_ref[r0:r0 + PEER_NKEYS, ls] = (act * gate).astype(BF16)
    o_ref[...] += lax.dot_general(act_ref[...], v_ref[...], (((0,), (0,)), ((), ())),
                                  preferred_element_type=F32)

    @pl.when(e == pl.num_programs(1) - 1)
    def _():
        x = x_ref[...] + o_ref[...]
        ms = jnp.mean(x * x, axis=-1, keepdims=True)
        o_ref[...] = x * lax.rsqrt(ms + EPS) * g_ref[...]


def _peer_experts(hn, a, b, stats, expert_u, expert_v, x, gain, *, t=512, eb=8 * PEER_NKEYS):
    n, d = hn.shape
    ne = expert_u.shape[0]
    nh = PEER_HEADS
    return pl.pallas_call(
        _peer_experts_kernel,
        grid=(n // t, ne // eb),
        in_specs=[
            pl.BlockSpec((t, d), lambda i, e: (i, 0)),
            pl.BlockSpec((nh, PEER_NKEYS, t), lambda i, e: (0, 0, i)),
            pl.BlockSpec((nh, PEER_NKEYS, t), lambda i, e: (0, 0, i)),
            pl.BlockSpec((nh, 8, t), lambda i, e: (0, 0, i)),
            pl.BlockSpec((eb, d), lambda i, e: (e, 0)),
            pl.BlockSpec((eb, d), lambda i, e: (e, 0)),
            pl.BlockSpec((t, d), lambda i, e: (i, 0)),
            pl.BlockSpec((1, d), lambda i, e: (0, 0)),
        ],
        out_specs=pl.BlockSpec((t, d), lambda i, e: (i, 0)),
        out_shape=jax.ShapeDtypeStruct((n, d), F32),
        scratch_shapes=[
            pltpu.VMEM((nh, PEER_NKEYS, t), F32),
            pltpu.VMEM((nh, PEER_NKEYS, t), F32),
            pltpu.VMEM((eb, t), F32),
            pltpu.VMEM((eb, t), BF16),
        ],
        compiler_params=_cparams("parallel", "arbitrary"),
        name="peer_experts",
    )(hn, a, b, stats, expert_u, expert_v, x, gain.reshape(1, d))


def _layer(x, mem, bsz, seq, p, final_gain):
    n, d = x.shape
    mlen = mem.shape[0] // bsz
    bf = lambda w: w.astype(BF16)
    w_in = p["w_in"]
    qkv_cols = 3 * ATT_WIDTH
    ubc_cols = qkv_cols + 3 * CONV_WIDTH

    qkv = _norm_matmul(x, p["norm_mix"], bf(w_in[:, :qkv_cols]), BF16)
    ubc = _norm_matmul(x, p["norm_mix"], bf(w_in[:, qkv_cols:ubc_cols]), F32)
    gates = _norm_matmul(x, p["norm_mix"], bf(w_in[:, ubc_cols:]), F32)

    att = _band_attention(qkv, _band_bias(p["rel_bias"]), bsz, seq)
    merged = _gated_merge(att, ubc, gates, p["conv_w"], p["conv_b"],
                          bf(p["w_att_out"]), bf(p["w_conv_out"]), seq)
    x = _matmul_residual(merged, bf(p["w_mix_out"]), x)

    q = _norm_matmul(x, p["norm_cross"], bf(p["w_cq"]), BF16)
    k = _norm_matmul(mem, p["norm_mem"], bf(p["w_ck"]), BF16)
    v = _norm_matmul(mem, p["norm_mem"], bf(p["w_cv"]), BF16)
    o = _cross_attention(q, k, v, bsz, seq, mlen)
    x = _matmul_residual(o, bf(p["w_co"]), x)

    pq, hn = _norm_matmul(x, p["norm_peer"], bf(p["w_pq"]), BF16, emit_h=True)
    a, b, stats = _peer_select(pq, bf(p["sub_keys"]))
    return _peer_experts(hn, a, b, stats, bf(p["expert_u"]), bf(p["expert_v"]), x, final_gain)


def kernel(x, mem, norm_mix, w_in, conv_w, conv_b, rel_bias, w_att_out, w_conv_out, w_mix_out,
           norm_cross, norm_mem, w_cq, w_ck, w_cv, w_co, norm_peer, w_pq, sub_keys,
           expert_u, expert_v, norm_final):
    bsz, seq, d = x.shape
    depth = w_in.shape[0]
    assert depth == 1, "the fused final rmsnorm assumes a single layer"
    params = dict(norm_mix=norm_mix[0], w_in=w_in[0], conv_w=conv_w[0], conv_b=conv_b[0],
                  rel_bias=rel_bias[0], w_att_out=w_att_out[0], w_conv_out=w_conv_out[0],
                  w_mix_out=w_mix_out[0], norm_cross=norm_cross[0], norm_mem=norm_mem[0],
                  w_cq=w_cq[0], w_ck=w_ck[0], w_cv=w_cv[0], w_co=w_co[0], norm_peer=norm_peer[0],
                  w_pq=w_pq[0], sub_keys=sub_keys[0], expert_u=expert_u[0], expert_v=expert_v[0])
    out = _layer(x.reshape(bsz * seq, d), mem.reshape(-1, d), bsz, seq, params, norm_final)
    return out.reshape(bsz, seq, d)
```

```python
import functools
import math

import jax
import jax.numpy as jnp
from jax import lax
from jax.experimental import pallas as pl
from jax.experimental.pallas import tpu as pltpu

F32 = jnp.float32
BF16 = jnp.bfloat16

CHUNK = 64
LEFT_CHUNKS = 8
ATT_HEADS = 8
ATT_HEAD_DIM = 128
ATT_WIDTH = ATT_HEADS * ATT_HEAD_DIM
REL_CLIP = 128
CONV_WIDTH = 1024
MEM_HEADS = 4
PEER_HEADS = 8
PEER_NKEYS = 128
PEER_HALF = 128
PEER_TOPK = 16
EPS = 1e-6
NEG_INF = -1e30

LANES = 128
ATT_QBLOCK = 512
ATT_SUB = 256
ATT_KWIN = ATT_SUB + LEFT_CHUNKS * CHUNK
VMEM_LIMIT = 56 * 1024 * 1024


def _cparams(*sem):
    return pltpu.CompilerParams(dimension_semantics=sem, vmem_limit_bytes=VMEM_LIMIT)


def _norm_matmul_kernel(x_ref, g_ref, w_ref, o_ref, *rest, emit_h):
    if emit_h:
        hout_ref, h_ref = rest
    else:
        (h_ref,) = rest

    @pl.when(pl.program_id(1) == 0)
    def _():
        x = x_ref[...]
        ms = jnp.mean(x * x, axis=-1, keepdims=True)
        h = (x * lax.rsqrt(ms + EPS) * g_ref[...]).astype(BF16)
        h_ref[...] = h
        if emit_h:
            hout_ref[...] = h.T

    o_ref[...] = jnp.dot(h_ref[...], w_ref[...], preferred_element_type=F32).astype(o_ref.dtype)


def _norm_matmul(x, gain, w, out_dtype, *, tm=1024, tn=1024, emit_h=False):
    n, d = x.shape
    c = w.shape[1]
    tm = min(tm, n)
    out_shape = [jax.ShapeDtypeStruct((n, c), out_dtype)]
    out_specs = [pl.BlockSpec((tm, tn), lambda i, j: (i, j))]
    if emit_h:
        out_shape.append(jax.ShapeDtypeStruct((d, n), BF16))
        out_specs.append(pl.BlockSpec((d, tm), lambda i, j: (0, i)))
    res = pl.pallas_call(
        functools.partial(_norm_matmul_kernel, emit_h=emit_h),
        grid=(n // tm, c // tn),
        in_specs=[
            pl.BlockSpec((tm, d), lambda i, j: (i, 0)),
            pl.BlockSpec((1, d), lambda i, j: (0, 0)),
            pl.BlockSpec((d, tn), lambda i, j: (0, j)),
        ],
        out_specs=out_specs,
        out_shape=out_shape,
        scratch_shapes=[pltpu.VMEM((tm, d), BF16)],
        compiler_params=_cparams("parallel", "arbitrary"),
        name="norm_matmul",
    )(x, gain.reshape(1, d), w)
    return res if emit_h else res[0]


def _matmul_residual_kernel(a_ref, w_ref, r_ref, o_ref):
    o_ref[...] = r_ref[...] + jnp.dot(a_ref[...], w_ref[...], preferred_element_type=F32)


def _matmul_residual(a, w, res, *, tm=1024, tn=1024):
    n, k = a.shape
    c = w.shape[1]
    return pl.pallas_call(
        _matmul_residual_kernel,
        grid=(n // tm, c // tn),
        in_specs=[
            pl.BlockSpec((tm, k), lambda i, j: (i, 0)),
            pl.BlockSpec((k, tn), lambda i, j: (0, j)),
            pl.BlockSpec((tm, tn), lambda i, j: (i, j)),
        ],
        out_specs=pl.BlockSpec((tm, tn), lambda i, j: (i, j)),
        out_shape=jax.ShapeDtypeStruct((n, c), F32),
        compiler_params=_cparams("parallel", "arbitrary"),
        name="matmul_residual",
    )(a, w, res)


def _band_attention_kernel(q_ref, kp_ref, kc_ref, vp_ref, vc_ref, bias_ref, o_ref, kwin_ref, vwin_ref):
    first = pl.program_id(1) == 0
    kwin_ref[0:ATT_QBLOCK, :] = kp_ref[...]
    kwin_ref[ATT_QBLOCK:, :] = kc_ref[...]
    vwin_ref[0:ATT_QBLOCK, :] = vp_ref[...]
    vwin_ref[ATT_QBLOCK:, :] = vc_ref[...]
    scale = ATT_HEAD_DIM ** -0.5
    col = lax.broadcasted_iota(jnp.int32, (ATT_SUB, ATT_KWIN), 1)
    for s in range(ATT_QBLOCK // ATT_SUB):
        r0 = s * ATT_SUB
        dead = jnp.logical_and(first, col < ATT_QBLOCK - r0)
        for h in range(ATT_HEADS):
            c0 = h * ATT_HEAD_DIM
            qh = q_ref[r0:r0 + ATT_SUB, c0:c0 + ATT_HEAD_DIM]
            kh = kwin_ref[r0:r0 + ATT_KWIN, c0:c0 + ATT_HEAD_DIM]
            vh = vwin_ref[r0:r0 + ATT_KWIN, c0:c0 + ATT_HEAD_DIM]
            sc = lax.dot_general(qh, kh, (((1,), (1,)), ((), ())), preferred_element_type=F32)
            sc = sc * scale + bias_ref[h]
            sc = jnp.where(dead, NEG_INF, sc)
            m = jnp.max(sc, axis=-1, keepdims=True)
            p = jnp.exp(sc - m)
            p = p / jnp.sum(p, axis=-1, keepdims=True)
            o = jnp.dot(p.astype(BF16), vh, preferred_element_type=F32)
            o_ref[r0:r0 + ATT_SUB, c0:c0 + ATT_HEAD_DIM] = o.astype(o_ref.dtype)


def _band_bias(rel_table):
    nh = rel_table.shape[0]
    r = jnp.arange(ATT_SUB)[:, None]
    c = jnp.arange(ATT_KWIN)[None, :]
    span = ATT_SUB + ATT_KWIN
    lead = ATT_SUB - 1 + LEFT_CHUNKS * CHUNK - REL_CLIP
    line = jnp.concatenate([
        jnp.broadcast_to(rel_table[:, -1:], (nh, lead)),
        rel_table[:, ::-1],
        jnp.broadcast_to(rel_table[:, :1], (nh, span - lead - 2 * REL_CLIP - 1)),
    ], axis=1)
    rows = jnp.tile(line, (1, ATT_SUB))[:, :ATT_SUB * (span - 1)].reshape(nh, ATT_SUB, span - 1)
    bias = rows[:, :, ATT_SUB - 1:ATT_SUB - 1 + ATT_KWIN]
    qc, kc = r // CHUNK, c // CHUNK
    in_band = jnp.logical_and(kc >= qc, kc <= qc + LEFT_CHUNKS)
    return jnp.where(in_band[None], bias, NEG_INF).astype(F32)


def _band_attention(qkv, bias, bsz, seq):
    n = bsz * seq
    nb = seq // ATT_QBLOCK
    blk = (ATT_QBLOCK, ATT_WIDTH)
    cur = lambda col: (lambda b, i: (b * nb + i, col))
    prev = lambda col: (lambda b, i: (b * nb + jnp.maximum(i - 1, 0), col))
    return pl.pallas_call(
        _band_attention_kernel,
        grid=(bsz, nb),
        in_specs=[
            pl.BlockSpec(blk, cur(0)),
            pl.BlockSpec(blk, prev(1)),
            pl.BlockSpec(blk, cur(1)),
            pl.BlockSpec(blk, prev(2)),
            pl.BlockSpec(blk, cur(2)),
            pl.BlockSpec((ATT_HEADS, ATT_SUB, ATT_KWIN), lambda b, i: (0, 0, 0)),
        ],
        out_specs=pl.BlockSpec(blk, lambda b, i: (b * nb + i, 0)),
        out_shape=jax.ShapeDtypeStruct((n, ATT_WIDTH), BF16),
        scratch_shapes=[pltpu.VMEM((2 * ATT_QBLOCK, ATT_WIDTH), BF16),
                        pltpu.VMEM((2 * ATT_QBLOCK, ATT_WIDTH), BF16)],
        compiler_params=_cparams("parallel", "arbitrary"),
        name="band_attention",
    )(qkv, qkv, qkv, qkv, qkv, bias)


def _gated_merge_kernel(att_ref, u_ref, b_ref, c_ref, uh_ref, ch_ref, cw_ref, cb_ref,
                        wa_ref, wc_ref, ga_ref, gb_ref, o_ref, yc_ref, *, tiles_per_seq):
    @pl.when(pl.program_id(1) == 0)
    def _():
        tm = u_ref.shape[0]
        z = c_ref[...] * u_ref[...]
        keep = (pl.program_id(0) % tiles_per_seq != 0).astype(F32)
        zh = ch_ref[...] * uh_ref[...] * keep
        row = lax.broadcasted_iota(jnp.int32, z.shape, 0)
        z1 = jnp.where(row == 0, zh[7:8], pltpu.roll(z, 1, 0))
        z2 = jnp.where(row == 0, zh[6:7], jnp.where(row == 1, zh[7:8], pltpu.roll(z, 2, 0)))
        y = z2 * cw_ref[0:1] + cb_ref[...]
        y = y + z1 * cw_ref[1:2]
        y = y + z * cw_ref[2:3]
        yc_ref[...] = (b_ref[...] * y).astype(BF16)

    y_att = jnp.dot(att_ref[...], wa_ref[...], preferred_element_type=F32)
    y_conv = jnp.dot(yc_ref[...], wc_ref[...], preferred_element_type=F32)
    merged = jax.nn.sigmoid(ga_ref[...]) * y_att + jax.nn.sigmoid(gb_ref[...]) * y_conv
    o_ref[...] = merged.astype(o_ref.dtype)


def _gated_merge(att, ubc, gates, conv_w, conv_b, w_att_out, w_conv_out, seq, *, tm=512, tn=1024):
    n = att.shape[0]
    d = w_att_out.shape[1]
    halo = 8
    hpt = tm // halo
    col = lambda cc: (lambda i, j: (i, cc))
    halo_map = lambda cc: (lambda i, j: (jnp.maximum(i * hpt - 1, 0), cc))
    nj = d // tn
    return pl.pallas_call(
        functools.partial(_gated_merge_kernel, tiles_per_seq=seq // tm),
        grid=(n // tm, nj),
        in_specs=[
            pl.BlockSpec((tm, ATT_WIDTH), col(0)),
            pl.BlockSpec((tm, CONV_WIDTH), col(0)),
            pl.BlockSpec((tm, CONV_WIDTH), col(1)),
            pl.BlockSpec((tm, CONV_WIDTH), col(2)),
            pl.BlockSpec((halo, CONV_WIDTH), halo_map(0)),
            pl.BlockSpec((halo, CONV_WIDTH), halo_map(2)),
            pl.BlockSpec((3, CONV_WIDTH), lambda i, j: (0, 0)),
            pl.BlockSpec((1, CONV_WIDTH), lambda i, j: (0, 0)),
            pl.BlockSpec((ATT_WIDTH, tn), lambda i, j: (0, j)),
            pl.BlockSpec((CONV_WIDTH, tn), lambda i, j: (0, j)),
            pl.BlockSpec((tm, tn), lambda i, j: (i, j)),
            pl.BlockSpec((tm, tn), lambda i, j: (i, j + nj)),
        ],
        out_specs=pl.BlockSpec((tm, tn), lambda i, j: (i, j)),
        out_shape=jax.ShapeDtypeStruct((n, d), BF16),
        scratch_shapes=[pltpu.VMEM((tm, CONV_WIDTH), BF16)],
        compiler_params=_cparams("parallel", "arbitrary"),
        name="gated_merge",
    )(att, ubc, ubc, ubc, ubc, ubc, conv_w, conv_b.reshape(1, CONV_WIDTH),
      w_att_out, w_conv_out, gates, gates)


def _cross_attention_kernel(q_ref, k_ref, v_ref, o_ref):
    dh = q_ref.shape[1] // MEM_HEADS
    scale = dh ** -0.5
    for h in range(MEM_HEADS):
        c0 = h * dh
        sc = lax.dot_general(q_ref[:, c0:c0 + dh], k_ref[:, c0:c0 + dh],
                             (((1,), (1,)), ((), ())), preferred_element_type=F32) * scale
        m = jnp.max(sc, axis=-1, keepdims=True)
        p = jnp.exp(sc - m)
        p = p / jnp.sum(p, axis=-1, keepdims=True)
        o = jnp.dot(p.astype(BF16), v_ref[:, c0:c0 + dh], preferred_element_type=F32)
        o_ref[:, c0:c0 + dh] = o.astype(o_ref.dtype)


def _cross_attention(q, k, v, bsz, seq, mlen, *, tm=512):
    n, d = q.shape
    nb = seq // tm
    return pl.pallas_call(
        _cross_attention_kernel,
        grid=(bsz, nb),
        in_specs=[
            pl.BlockSpec((tm, d), lambda b, i: (b * nb + i, 0)),
            pl.BlockSpec((mlen, d), lambda b, i: (b, 0)),
            pl.BlockSpec((mlen, d), lambda b, i: (b, 0)),
        ],
        out_specs=pl.BlockSpec((tm, d), lambda b, i: (b * nb + i, 0)),
        out_shape=jax.ShapeDtypeStruct((n, d), BF16),
        compiler_params=_cparams("parallel", "arbitrary"),
        name="cross_attention",
    )(q, k, v)


def _batcher_network(n):
    def merge(lo, hi, r):
        step = r * 2
        if step < hi - lo:
            yield from merge(lo, hi, step)
            yield from merge(lo + r, hi, step)
            yield from [(i, i + r) for i in range(lo + r, hi - r, step)]
        else:
            yield (lo, lo + r)

    def sort(lo, hi):
        if hi - lo >= 1:
            mid = lo + (hi - lo) // 2
            yield from sort(lo, mid)
            yield from sort(mid + 1, hi)
            yield from merge(lo, hi, 1)

    return list(sort(0, n - 1))


def _bitonic_network(n):
    pairs, d = [], n // 2
    while d >= 1:
        pairs += [(i, i + d) for i in range(n) if not i & d]
        d //= 2
    return pairs


def _apply_network(v, pairs):
    v = list(v)
    for i, j in pairs:
        v[i], v[j] = jnp.maximum(v[i], v[j]), jnp.minimum(v[i], v[j])
    return v


def _merge_sublanes(v, shift):
    n = len(v)
    w = [pltpu.roll(x, shift, 0) for x in v]
    return _apply_network([jnp.maximum(v[k], w[n - 1 - k]) for k in range(n)], _bitonic_network(n))


def _top16_over_keys(ref, h, lanes):
    v = [ref[h, 8 * k:8 * k + 8, lanes] for k in range(PEER_NKEYS // 8)]
    v = _apply_network(v, _batcher_network(len(v)))
    for shift in (4, 2, 1):
        v = _merge_sublanes(v, shift)
    return v


def _spread(v, sub):
    out = v[0]
    for k in range(1, 8):
        out = jnp.where(sub == k, v[k], out)
    return out


def _peer_select_kernel(q_ref, sk_ref, a_ref, b_ref, st_ref):
    t = q_ref.shape[0]
    for h in range(PEER_HEADS):
        c0 = h * 2 * PEER_HALF
        dn = (((1,), (1,)), ((), ()))
        a_ref[h] = lax.dot_general(sk_ref[h, 0], q_ref[:, c0:c0 + PEER_HALF], dn,
                                   preferred_element_type=F32)
        b_ref[h] = lax.dot_general(sk_ref[h, 1], q_ref[:, c0 + PEER_HALF:c0 + 2 * PEER_HALF], dn,
                                   preferred_element_type=F32)

    lane_tiles = t // LANES
    sub = lax.broadcasted_iota(jnp.int32, (8, LANES), 0)
    neg = -jnp.inf

    def body(it, carry):
        h = it // lane_tiles
        lanes = pl.ds(pl.multiple_of((it % lane_tiles) * LANES, LANES), LANES)
        a = _top16_over_keys(a_ref, h, lanes)
        b = _top16_over_keys(b_ref, h, lanes)
        a_lo, a_hi = _spread(a[:8], sub), _spread(a[8:], sub)
        b_lo, b_hi = _spread(b[:8], sub), _spread(b[8:], sub)
        cand = [
            a[0] + b_lo,
            a[0] + b_hi,
            a[1] + b_lo,
            jnp.where(sub >= 2, a_lo + b[0],
                      jnp.where(sub == 0, a[4] + b[2], neg)),
            a_hi + b[0],
            jnp.where(sub >= 2, a_lo + b[1], neg),
            jnp.where(sub >= 2, a[2] + b_lo, neg),
            jnp.where(sub >= 2, a[3] + b_lo, neg),
        ]
        cand = _apply_network(cand, _batcher_network(8))
        mirror = [pltpu.roll(x, 4, 0) for x in cand]
        best = _apply_network(cand + mirror[::-1], _bitonic_network(PEER_TOPK))
        for shift in (2, 1):
            best = _merge_sublanes(best, shift)
        z = jnp.zeros((8, LANES), F32)
        for v in best:
            z = z + jnp.exp(v - best[0])
        thr = best[PEER_TOPK - 1]
        log_norm = a[0] + jnp.log(z)
        st_ref[h, :, lanes] = jnp.where(sub == 0, thr, jnp.where(sub == 1, log_norm,
                                                                 jnp.where(sub == 2, b[0], 0.0)))
        return carry

    lax.fori_loop(0, PEER_HEADS * lane_tiles, body, 0)


def _peer_select(q, sub_keys, *, t=512):
    n = q.shape[0]
    nh = PEER_HEADS
    score = jax.ShapeDtypeStruct((nh, PEER_NKEYS, n), F32)
    return pl.pallas_call(
        _peer_select_kernel,
        grid=(n // t,),
        in_specs=[
            pl.BlockSpec((t, q.shape[1]), lambda i: (i, 0)),
            pl.BlockSpec(sub_keys.shape, lambda i: (0, 0, 0, 0)),
        ],
        out_specs=[
            pl.BlockSpec((nh, PEER_NKEYS, t), lambda i: (0, 0, i)),
            pl.BlockSpec((nh, PEER_NKEYS, t), lambda i: (0, 0, i)),
            pl.BlockSpec((nh, 8, t), lambda i: (0, 0, i)),
        ],
        out_shape=[score, score, jax.ShapeDtypeStruct((nh, 8, n), F32)],
        compiler_params=_cparams("parallel"),
        name="peer_select",
    )(q, sub_keys)


def _peer_experts_kernel(ht_ref, a_ref, b_ref, st_ref, u_ref, v_ref, o_ref,
                         eb_ref, w_ref, hu_ref, gate_ref, act_ref, *, n_blocks):
    e = pl.program_id(1)
    t = ht_ref.shape[1]
    eb = u_ref.shape[0]
    assert eb == 8 * PEER_NKEYS
    apply_dn = (((0,), (0,)), ((), ()))

    @pl.when(e == 0)
    def _():
        o_ref[...] = jnp.zeros_like(o_ref)
        act_ref[1] = jnp.zeros((eb, t), BF16)
        for h in range(PEER_HEADS):
            eb_ref[h] = jnp.exp(b_ref[h] - st_ref[h, 2:3, :])
            w_ref[h] = jnp.exp(a_ref[h] - st_ref[h, 1:2, :])

    @pl.when(e < n_blocks)
    def _():
        slot = e % 2
        hu_ref[...] = jnp.dot(u_ref[...], ht_ref[...], preferred_element_type=F32)
        i0 = pl.multiple_of(e * 8, 8)
        for lt in range(t // LANES):
            ls = slice(lt * LANES, (lt + 1) * LANES)
            a_rows = [a_ref[h, pl.ds(i0, 8), ls] for h in range(PEER_HEADS)]
            w_rows = [w_ref[h, pl.ds(i0, 8), ls] for h in range(PEER_HEADS)]
            for ii in range(8):
                gate = jnp.zeros((PEER_NKEYS, LANES), F32)
                for h in range(PEER_HEADS):
                    pair = b_ref[h, :, ls] + a_rows[h][ii:ii + 1]
                    sel = pair >= st_ref[h, 0:1, ls]
                    gate = gate + jnp.where(sel, eb_ref[h, :, ls], 0.0) * w_rows[h][ii:ii + 1]
                gate_ref[ii * PEER_NKEYS:(ii + 1) * PEER_NKEYS, ls] = gate
        o_ref[...] += lax.dot_general(act_ref[1 - slot], v_ref[...], apply_dn,
                                      preferred_element_type=F32)
        hu = hu_ref[...]
        act = 0.5 * hu * (1.0 + lax.erf(hu * (2.0 ** -0.5)))
        act_ref[slot] = (act * gate_ref[...]).astype(BF16)

    @pl.when(e == n_blocks)
    def _():
        o_ref[...] += lax.dot_general(act_ref[(n_blocks - 1) % 2], v_ref[...], apply_dn,
                                      preferred_element_type=F32)


def _peer_experts(ht, a, b, stats, expert_u, expert_v, *, t=512, eb=8 * PEER_NKEYS):
    d, n = ht.shape
    nblk = expert_u.shape[0] // eb
    nh = PEER_HEADS
    return pl.pallas_call(
        functools.partial(_peer_experts_kernel, n_blocks=nblk),
        grid=(n // t, nblk + 1),
        in_specs=[
            pl.BlockSpec((d, t), lambda i, e: (0, i)),
            pl.BlockSpec((nh, PEER_NKEYS, t), lambda i, e: (0, 0, i)),
            pl.BlockSpec((nh, PEER_NKEYS, t), lambda i, e: (0, 0, i)),
            pl.BlockSpec((nh, 8, t), lambda i, e: (0, 0, i)),
            pl.BlockSpec((eb, d), lambda i, e: (jnp.minimum(e, nblk - 1), 0)),
            pl.BlockSpec((eb, d), lambda i, e: (jnp.maximum(e - 1, 0), 0)),
        ],
        out_specs=pl.BlockSpec((t, d), lambda i, e: (i, 0)),
        out_shape=jax.ShapeDtypeStruct((n, d), F32),
        scratch_shapes=[
            pltpu.VMEM((nh, PEER_NKEYS, t), F32),
            pltpu.VMEM((nh, PEER_NKEYS, t), F32),
            pltpu.VMEM((eb, t), F32),
            pltpu.VMEM((eb, t), F32),
            pltpu.VMEM((2, eb, t), BF16),
        ],
        compiler_params=_cparams("parallel", "arbitrary"),
        name="peer_experts",
    )(ht, a, b, stats, expert_u, expert_v)


def _residual_rmsnorm_kernel(x_ref, y_ref, g_ref, o_ref):
    x = x_ref[...] + y_ref[...]
    ms = jnp.mean(x * x, axis=-1, keepdims=True)
    o_ref[...] = x * lax.rsqrt(ms + EPS) * g_ref[...]


def _residual_rmsnorm(x, y, gain, *, tm=512):
    n, d = x.shape
    row = pl.BlockSpec((tm, d), lambda i: (i, 0))
    return pl.pallas_call(
        _residual_rmsnorm_kernel,
        grid=(n // tm,),
        in_specs=[row, row, pl.BlockSpec((1, d), lambda i: (0, 0))],
        out_specs=row,
        out_shape=jax.ShapeDtypeStruct((n, d), F32),
        compiler_params=_cparams("parallel"),
        name="residual_rmsnorm",
    )(x, y, gain.reshape(1, d))


def _layer(x, mem, bsz, seq, p, final_gain):
    n, d = x.shape
    mlen = mem.shape[0] // bsz
    bf = lambda w: w.astype(BF16)
    w_in = p["w_in"]
    qkv_cols = 3 * ATT_WIDTH
    ubc_cols = qkv_cols + 3 * CONV_WIDTH

    qkv = _norm_matmul(x, p["norm_mix"], bf(w_in[:, :qkv_cols]), BF16)
    ubc = _norm_matmul(x, p["norm_mix"], bf(w_in[:, qkv_cols:ubc_cols]), F32)
    gates = _norm_matmul(x, p["norm_mix"], bf(w_in[:, ubc_cols:]), F32)

    att = _band_attention(qkv, _band_bias(p["rel_bias"]), bsz, seq)
    merged = _gated_merge(att, ubc, gates, p["conv_w"], p["conv_b"],
                          bf(p["w_att_out"]), bf(p["w_conv_out"]), seq)
    x = _matmul_residual(merged, bf(p["w_mix_out"]), x)

    q = _norm_matmul(x, p["norm_cross"], bf(p["w_cq"]), BF16)
    k = _norm_matmul(mem, p["norm_mem"], bf(p["w_ck"]), BF16)
    v = _norm_matmul(mem, p["norm_mem"], bf(p["w_cv"]), BF16)
    o = _cross_attention(q, k, v, bsz, seq, mlen)
    x = _matmul_residual(o, bf(p["w_co"]), x)

    pq, ht = _norm_matmul(x, p["norm_peer"], bf(p["w_pq"]), BF16, emit_h=True)
    a, b, stats = _peer_select(pq, bf(p["sub_keys"]))
    y = _peer_experts(ht, a, b, stats, bf(p["expert_u"]), bf(p["expert_v"]))
    return _residual_rmsnorm(x, y, final_gain)


def kernel(x, mem, norm_mix, w_in, conv_w, conv_b, rel_bias, w_att_out, w_conv_out, w_mix_out,
           norm_cross, norm_mem, w_cq, w_ck, w_cv, w_co, norm_peer, w_pq, sub_keys,
           expert_u, expert_v, norm_final):
    bsz, seq, d = x.shape
    depth = w_in.shape[0]
    assert depth == 1, "the fused final rmsnorm assumes a single layer"
    params = dict(norm_mix=norm_mix[0], w_in=w_in[0], conv_w=conv_w[0], conv_b=conv_b[0],
                  rel_bias=rel_bias[0], w_att_out=w_att_out[0], w_conv_out=w_conv_out[0],
                  w_mix_out=w_mix_out[0], norm_cross=norm_cross[0], norm_mem=norm_mem[0],
                  w_cq=w_cq[0], w_ck=w_ck[0], w_cv=w_cv[0], w_co=w_co[0], norm_peer=norm_peer[0],
                  w_pq=w_pq[0], sub_keys=sub_keys[0], expert_u=expert_u[0], expert_v=expert_v[0])
    out = _layer(x.reshape(bsz * seq, d), mem.reshape(-1, d), bsz, seq, params, norm_final)
    return out.reshape(bsz, seq, d)
```

```python
import functools
import math

import jax
import jax.numpy as jnp
from jax import lax
from jax.experimental import pallas as pl
from jax.experimental.pallas import tpu as pltpu

F32 = jnp.float32
BF16 = jnp.bfloat16

CHUNK = 64
LEFT_CHUNKS = 8
ATT_HEADS = 8
ATT_HEAD_DIM = 128
ATT_WIDTH = ATT_HEADS * ATT_HEAD_DIM
REL_CLIP = 128
CONV_WIDTH = 1024
MEM_HEADS = 4
PEER_HEADS = 8
PEER_NKEYS = 128
PEER_HALF = 128
PEER_TOPK = 16
EPS = 1e-6
NEG_INF = -1e30

LANES = 128
ATT_QBLOCK = 512
ATT_SUB = 256
ATT_KWIN = ATT_SUB + LEFT_CHUNKS * CHUNK
VMEM_LIMIT = 56 * 1024 * 1024


def _cparams(*sem, flags=None):
    return pltpu.CompilerParams(dimension_semantics=sem, vmem_limit_bytes=VMEM_LIMIT, flags=flags)


def _norm_matmul_kernel(x_ref, g_ref, w_ref, o_ref, *rest, emit_h):
    if emit_h:
        hout_ref, h_ref = rest
    else:
        (h_ref,) = rest

    @pl.when(pl.program_id(1) == 0)
    def _():
        x = x_ref[...]
        ms = jnp.mean(x * x, axis=-1, keepdims=True)
        h = (x * lax.rsqrt(ms + EPS) * g_ref[...]).astype(BF16)
        h_ref[...] = h
        if emit_h:
            hout_ref[...] = h.T

    o_ref[...] = jnp.dot(h_ref[...], w_ref[...], preferred_element_type=F32).astype(o_ref.dtype)


def _norm_matmul(x, gain, w, out_dtype, *, tm=1024, tn=1024, emit_h=False):
    n, d = x.shape
    c = w.shape[1]
    tm = min(tm, n)
    out_shape = [jax.ShapeDtypeStruct((n, c), out_dtype)]
    out_specs = [pl.BlockSpec((tm, tn), lambda i, j: (i, j))]
    if emit_h:
        out_shape.append(jax.ShapeDtypeStruct((d, n), BF16))
        out_specs.append(pl.BlockSpec((d, tm), lambda i, j: (0, i)))
    res = pl.pallas_call(
        functools.partial(_norm_matmul_kernel, emit_h=emit_h),
        grid=(n // tm, c // tn),
        in_specs=[
            pl.BlockSpec((tm, d), lambda i, j: (i, 0)),
            pl.BlockSpec((1, d), lambda i, j: (0, 0)),
            pl.BlockSpec((d, tn), lambda i, j: (0, j)),
        ],
        out_specs=out_specs,
        out_shape=out_shape,
        scratch_shapes=[pltpu.VMEM((tm, d), BF16)],
        compiler_params=_cparams("parallel", "arbitrary"),
        name="norm_matmul",
    )(x, gain.reshape(1, d), w)
    return res if emit_h else res[0]


def _matmul_residual_kernel(a_ref, w_ref, r_ref, o_ref):
    o_ref[...] = r_ref[...] + jnp.dot(a_ref[...], w_ref[...], preferred_element_type=F32)


def _matmul_residual(a, w, res, *, tm=1024, tn=1024):
    n, k = a.shape
    c = w.shape[1]
    return pl.pallas_call(
        _matmul_residual_kernel,
        grid=(n // tm, c // tn),
        in_specs=[
            pl.BlockSpec((tm, k), lambda i, j: (i, 0)),
            pl.BlockSpec((k, tn), lambda i, j: (0, j)),
            pl.BlockSpec((tm, tn), lambda i, j: (i, j)),
        ],
        out_specs=pl.BlockSpec((tm, tn), lambda i, j: (i, j)),
        out_shape=jax.ShapeDtypeStruct((n, c), F32),
        compiler_params=_cparams("parallel", "arbitrary"),
        name="matmul_residual",
    )(a, w, res)


def _band_attention_kernel(q_ref, kp_ref, kc_ref, vp_ref, vc_ref, bias_ref, o_ref, kwin_ref, vwin_ref):
    first = pl.program_id(1) == 0
    kwin_ref[0:ATT_QBLOCK, :] = kp_ref[...]
    kwin_ref[ATT_QBLOCK:, :] = kc_ref[...]
    vwin_ref[0:ATT_QBLOCK, :] = vp_ref[...]
    vwin_ref[ATT_QBLOCK:, :] = vc_ref[...]
    scale = ATT_HEAD_DIM ** -0.5
    col = lax.broadcasted_iota(jnp.int32, (ATT_SUB, ATT_KWIN), 1)
    for s in range(ATT_QBLOCK // ATT_SUB):
        r0 = s * ATT_SUB
        dead = jnp.logical_and(first, col < ATT_QBLOCK - r0)
        for h in range(ATT_HEADS):
            c0 = h * ATT_HEAD_DIM
            qh = q_ref[r0:r0 + ATT_SUB, c0:c0 + ATT_HEAD_DIM]
            kh = kwin_ref[r0:r0 + ATT_KWIN, c0:c0 + ATT_HEAD_DIM]
            vh = vwin_ref[r0:r0 + ATT_KWIN, c0:c0 + ATT_HEAD_DIM]
            sc = lax.dot_general(qh, kh, (((1,), (1,)), ((), ())), preferred_element_type=F32)
            sc = sc * scale + bias_ref[h]
            sc = jnp.where(dead, NEG_INF, sc)
            m = jnp.max(sc, axis=-1, keepdims=True)
            p = jnp.exp(sc - m)
            p = p / jnp.sum(p, axis=-1, keepdims=True)
            o = jnp.dot(p.astype(BF16), vh, preferred_element_type=F32)
            o_ref[r0:r0 + ATT_SUB, c0:c0 + ATT_HEAD_DIM] = o.astype(o_ref.dtype)


def _band_bias(rel_table):
    nh = rel_table.shape[0]
    r = jnp.arange(ATT_SUB)[:, None]
    c = jnp.arange(ATT_KWIN)[None, :]
    span = ATT_SUB + ATT_KWIN
    lead = ATT_SUB - 1 + LEFT_CHUNKS * CHUNK - REL_CLIP
    line = jnp.concatenate([
        jnp.broadcast_to(rel_table[:, -1:], (nh, lead)),
        rel_table[:, ::-1],
        jnp.broadcast_to(rel_table[:, :1], (nh, span - lead - 2 * REL_CLIP - 1)),
    ], axis=1)
    rows = jnp.tile(line, (1, ATT_SUB))[:, :ATT_SUB * (span - 1)].reshape(nh, ATT_SUB, span - 1)
    bias = rows[:, :, ATT_SUB - 1:ATT_SUB - 1 + ATT_KWIN]
    qc, kc = r // CHUNK, c // CHUNK
    in_band = jnp.logical_and(kc >= qc, kc <= qc + LEFT_CHUNKS)
    return jnp.where(in_band[None], bias, NEG_INF).astype(F32)


def _band_attention(qkv, bias, bsz, seq):
    n = bsz * seq
    nb = seq // ATT_QBLOCK
    blk = (ATT_QBLOCK, ATT_WIDTH)
    cur = lambda col: (lambda b, i: (b * nb + i, col))
    prev = lambda col: (lambda b, i: (b * nb + jnp.maximum(i - 1, 0), col))
    return pl.pallas_call(
        _band_attention_kernel,
        grid=(bsz, nb),
        in_specs=[
            pl.BlockSpec(blk, cur(0)),
            pl.BlockSpec(blk, prev(1)),
            pl.BlockSpec(blk, cur(1)),
            pl.BlockSpec(blk, prev(2)),
            pl.BlockSpec(blk, cur(2)),
            pl.BlockSpec((ATT_HEADS, ATT_SUB, ATT_KWIN), lambda b, i: (0, 0, 0)),
        ],
        out_specs=pl.BlockSpec(blk, lambda b, i: (b * nb + i, 0)),
        out_shape=jax.ShapeDtypeStruct((n, ATT_WIDTH), BF16),
        scratch_shapes=[pltpu.VMEM((2 * ATT_QBLOCK, ATT_WIDTH), BF16),
                        pltpu.VMEM((2 * ATT_QBLOCK, ATT_WIDTH), BF16)],
        compiler_params=_cparams("parallel", "arbitrary"),
        name="band_attention",
    )(qkv, qkv, qkv, qkv, qkv, bias)


def _gated_merge_kernel(att_ref, u_ref, b_ref, c_ref, uh_ref, ch_ref, cw_ref, cb_ref,
                        wa_ref, wc_ref, ga_ref, gb_ref, o_ref, yc_ref, *, tiles_per_seq):
    @pl.when(pl.program_id(1) == 0)
    def _():
        tm = u_ref.shape[0]
        z = c_ref[...] * u_ref[...]
        keep = (pl.program_id(0) % tiles_per_seq != 0).astype(F32)
        zh = ch_ref[...] * uh_ref[...] * keep
        row = lax.broadcasted_iota(jnp.int32, z.shape, 0)
        z1 = jnp.where(row == 0, zh[7:8], pltpu.roll(z, 1, 0))
        z2 = jnp.where(row == 0, zh[6:7], jnp.where(row == 1, zh[7:8], pltpu.roll(z, 2, 0)))
        y = z2 * cw_ref[0:1] + cb_ref[...]
        y = y + z1 * cw_ref[1:2]
        y = y + z * cw_ref[2:3]
        yc_ref[...] = (b_ref[...] * y).astype(BF16)

    y_att = jnp.dot(att_ref[...], wa_ref[...], preferred_element_type=F32)
    y_conv = jnp.dot(yc_ref[...], wc_ref[...], preferred_element_type=F32)
    merged = jax.nn.sigmoid(ga_ref[...]) * y_att + jax.nn.sigmoid(gb_ref[...]) * y_conv
    o_ref[...] = merged.astype(o_ref.dtype)


def _gated_merge(att, ubc, gates, conv_w, conv_b, w_att_out, w_conv_out, seq, *, tm=512, tn=1024):
    n = att.shape[0]
    d = w_att_out.shape[1]
    halo = 8
    hpt = tm // halo
    col = lambda cc: (lambda i, j: (i, cc))
    halo_map = lambda cc: (lambda i, j: (jnp.maximum(i * hpt - 1, 0), cc))
    nj = d // tn
    return pl.pallas_call(
        functools.partial(_gated_merge_kernel, tiles_per_seq=seq // tm),
        grid=(n // tm, nj),
        in_specs=[
            pl.BlockSpec((tm, ATT_WIDTH), col(0)),
            pl.BlockSpec((tm, CONV_WIDTH), col(0)),
            pl.BlockSpec((tm, CONV_WIDTH), col(1)),
            pl.BlockSpec((tm, CONV_WIDTH), col(2)),
            pl.BlockSpec((halo, CONV_WIDTH), halo_map(0)),
            pl.BlockSpec((halo, CONV_WIDTH), halo_map(2)),
            pl.BlockSpec((3, CONV_WIDTH), lambda i, j: (0, 0)),
            pl.BlockSpec((1, CONV_WIDTH), lambda i, j: (0, 0)),
            pl.BlockSpec((ATT_WIDTH, tn), lambda i, j: (0, j)),
            pl.BlockSpec((CONV_WIDTH, tn), lambda i, j: (0, j)),
            pl.BlockSpec((tm, tn), lambda i, j: (i, j)),
            pl.BlockSpec((tm, tn), lambda i, j: (i, j + nj)),
        ],
        out_specs=pl.BlockSpec((tm, tn), lambda i, j: (i, j)),
        out_shape=jax.ShapeDtypeStruct((n, d), BF16),
        scratch_shapes=[pltpu.VMEM((tm, CONV_WIDTH), BF16)],
        compiler_params=_cparams("parallel", "arbitrary"),
        name="gated_merge",
    )(att, ubc, ubc, ubc, ubc, ubc, conv_w, conv_b.reshape(1, CONV_WIDTH),
      w_att_out, w_conv_out, gates, gates)


def _cross_attention_kernel(q_ref, k_ref, v_ref, o_ref):
    dh = q_ref.shape[1] // MEM_HEADS
    scale = dh ** -0.5
    for h in range(MEM_HEADS):
        c0 = h * dh
        sc = lax.dot_general(q_ref[:, c0:c0 + dh], k_ref[:, c0:c0 + dh],
                             (((1,), (1,)), ((), ())), preferred_element_type=F32) * scale
        m = jnp.max(sc, axis=-1, keepdims=True)
        p = jnp.exp(sc - m)
        p = p / jnp.sum(p, axis=-1, keepdims=True)
        o = jnp.dot(p.astype(BF16), v_ref[:, c0:c0 + dh], preferred_element_type=F32)
        o_ref[:, c0:c0 + dh] = o.astype(o_ref.dtype)


def _cross_attention(q, k, v, bsz, seq, mlen, *, tm=512):
    n, d = q.shape
    nb = seq // tm
    return pl.pallas_call(
        _cross_attention_kernel,
        grid=(bsz, nb),
        in_specs=[
            pl.BlockSpec((tm, d), lambda b, i: (b * nb + i, 0)),
            pl.BlockSpec((mlen, d), lambda b, i: (b, 0)),
            pl.BlockSpec((mlen, d), lambda b, i: (b, 0)),
        ],
        out_specs=pl.BlockSpec((tm, d), lambda b, i: (b * nb + i, 0)),
        out_shape=jax.ShapeDtypeStruct((n, d), BF16),
        compiler_params=_cparams("parallel", "arbitrary"),
        name="cross_attention",
    )(q, k, v)


def _batcher_network(n):
    def merge(lo, hi, r):
        step = r * 2
        if step < hi - lo:
            yield from merge(lo, hi, step)
            yield from merge(lo + r, hi, step)
            yield from [(i, i + r) for i in range(lo + r, hi - r, step)]
        else:
            yield (lo, lo + r)

    def sort(lo, hi):
        if hi - lo >= 1:
            mid = lo + (hi - lo) // 2
            yield from sort(lo, mid)
            yield from sort(mid + 1, hi)
            yield from merge(lo, hi, 1)

    return list(sort(0, n - 1))


def _bitonic_network(n):
    pairs, d = [], n // 2
    while d >= 1:
        pairs += [(i, i + d) for i in range(n) if not i & d]
        d //= 2
    return pairs


def _apply_network(v, pairs):
    v = list(v)
    for i, j in pairs:
        v[i], v[j] = jnp.maximum(v[i], v[j]), jnp.minimum(v[i], v[j])
    return v


def _merge_sublanes(v, shift):
    n = len(v)
    w = [pltpu.roll(x, shift, 0) for x in v]
    return _apply_network([jnp.maximum(v[k], w[n - 1 - k]) for k in range(n)], _bitonic_network(n))


def _top16_over_keys(ref, h, lanes):
    v = [ref[h, 8 * k:8 * k + 8, lanes] for k in range(PEER_NKEYS // 8)]
    v = _apply_network(v, _batcher_network(len(v)))
    for shift in (4, 2, 1):
        v = _merge_sublanes(v, shift)
    return v


def _spread(v, sub):
    out = v[0]
    for k in range(1, 8):
        out = jnp.where(sub == k, v[k], out)
    return out


def _peer_select_kernel(q_ref, sk_ref, c_ref, w_ref, b_ref, st_ref, a_ref):
    t = q_ref.shape[0]
    for h in range(PEER_HEADS):
        c0 = h * 2 * PEER_HALF
        dn = (((1,), (1,)), ((), ()))
        a_ref[h] = lax.dot_general(sk_ref[h, 0], q_ref[:, c0:c0 + PEER_HALF], dn,
                                   preferred_element_type=F32)
        b_ref[h] = lax.dot_general(sk_ref[h, 1], q_ref[:, c0 + PEER_HALF:c0 + 2 * PEER_HALF], dn,
                                   preferred_element_type=F32)

    lane_tiles = t // LANES
    sub = lax.broadcasted_iota(jnp.int32, (8, LANES), 0)
    neg = -jnp.inf

    def body(it, carry):
        h = it // lane_tiles
        lanes = pl.ds(pl.multiple_of((it % lane_tiles) * LANES, LANES), LANES)
        a = _top16_over_keys(a_ref, h, lanes)
        b = _top16_over_keys(b_ref, h, lanes)
        a_lo, a_hi = _spread(a[:8], sub), _spread(a[8:], sub)
        b_lo, b_hi = _spread(b[:8], sub), _spread(b[8:], sub)
        cand = [
            a[0] + b_lo,
            a[0] + b_hi,
            a[1] + b_lo,
            jnp.where(sub >= 2, a_lo + b[0],
                      jnp.where(sub == 0, a[4] + b[2], neg)),
            a_hi + b[0],
            jnp.where(sub >= 2, a_lo + b[1], neg),
            jnp.where(sub >= 2, a[2] + b_lo, neg),
            jnp.where(sub >= 2, a[3] + b_lo, neg),
        ]
        cand = _apply_network(cand, _batcher_network(8))
        mirror = [pltpu.roll(x, 4, 0) for x in cand]
        best = _apply_network(cand + mirror[::-1], _bitonic_network(PEER_TOPK))
        for shift in (2, 1):
            best = _merge_sublanes(best, shift)
        z = jnp.zeros((8, LANES), F32)
        for v in best:
            z = z + jnp.exp(v - best[0])
        thr = best[PEER_TOPK - 1]
        log_norm = a[0] + jnp.log(z)
        st_ref[h, :, lanes] = b[0]
        for k in range(PEER_NKEYS // 8):
            rows = slice(8 * k, 8 * k + 8)
            a_k = a_ref[h, rows, lanes]
            cut = jnp.full((8, LANES), jnp.inf, F32)
            for r in range(PEER_TOPK):
                cut = jnp.where(a_k + b[r] >= thr, b[r], cut)
            c_ref[h, rows, lanes] = cut
            w_ref[h, rows, lanes] = 0.5 * jnp.exp(a_k - log_norm)
        return carry

    lax.fori_loop(0, PEER_HEADS * lane_tiles, body, 0)


def _peer_select(q, sub_keys, *, t=512):
    n = q.shape[0]
    nh = PEER_HEADS
    score = jax.ShapeDtypeStruct((nh, PEER_NKEYS, n), F32)
    score_spec = pl.BlockSpec((nh, PEER_NKEYS, t), lambda i: (0, 0, i))
    return pl.pallas_call(
        _peer_select_kernel,
        grid=(n // t,),
        in_specs=[
            pl.BlockSpec((t, q.shape[1]), lambda i: (i, 0)),
            pl.BlockSpec(sub_keys.shape, lambda i: (0, 0, 0, 0)),
        ],
        out_specs=[score_spec, score_spec, score_spec,
                   pl.BlockSpec((nh, 8, t), lambda i: (0, 0, i))],
        out_shape=[score, score, score, jax.ShapeDtypeStruct((nh, 8, n), F32)],
        scratch_shapes=[pltpu.VMEM((nh, PEER_NKEYS, t), F32)],
        compiler_params=_cparams("parallel"),
        name="peer_select",
    )(q, sub_keys)


def _peer_experts_kernel(ht_ref, c_ref, w_ref, b_ref, st_ref, u_ref, v_ref, o_ref,
                         eb_ref, hu_ref, gate_ref, act_ref, *, n_blocks):
    e = pl.program_id(1)
    t = ht_ref.shape[1]
    eb = u_ref.shape[0]
    assert eb == 8 * PEER_NKEYS
    apply_dn = (((0,), (0,)), ((), ()))

    @pl.when(e == 0)
    def _():
        o_ref[...] = jnp.zeros_like(o_ref)
        act_ref[1] = jnp.zeros((eb, t), BF16)
        for h in range(PEER_HEADS):
            eb_ref[h] = jnp.exp(b_ref[h] - st_ref[h, 0:1, :])

    @pl.when(e < n_blocks)
    def _():
        slot = e % 2
        hu_ref[...] = jnp.dot(u_ref[...], ht_ref[...], preferred_element_type=F32)
        i0 = pl.multiple_of(e * 8, 8)
        for lt in range(t // LANES):
            ls = slice(lt * LANES, (lt + 1) * LANES)
            c_rows = [c_ref[h, pl.ds(i0, 8), ls] for h in range(PEER_HEADS)]
            w_rows = [w_ref[h, pl.ds(i0, 8), ls] for h in range(PEER_HEADS)]
            for ii in range(8):
                gate = None
                for h in range(PEER_HEADS):
                    sel = b_ref[h, :, ls] >= c_rows[h][ii:ii + 1]
                    term = jnp.where(sel, eb_ref[h, :, ls], 0.0) * w_rows[h][ii:ii + 1]
                    gate = term if gate is None else gate + term
                gate_ref[ii * PEER_NKEYS:(ii + 1) * PEER_NKEYS, ls] = gate
        o_ref[...] += lax.dot_general(act_ref[1 - slot], v_ref[...], apply_dn,
                                      preferred_element_type=F32)
        hu = hu_ref[...]
        act = hu * (1.0 + lax.erf(hu * (2.0 ** -0.5)))
        act_ref[slot] = (act * gate_ref[...]).astype(BF16)

    @pl.when(e == n_blocks)
    def _():
        o_ref[...] += lax.dot_general(act_ref[(n_blocks - 1) % 2], v_ref[...], apply_dn,
                                      preferred_element_type=F32)


def _peer_experts(ht, c, w, b, stats, expert_u, expert_v, *, t=512, eb=8 * PEER_NKEYS):
    d, n = ht.shape
    nblk = expert_u.shape[0] // eb
    nh = PEER_HEADS
    score = pl.BlockSpec((nh, PEER_NKEYS, t), lambda i, e: (0, 0, i))
    return pl.pallas_call(
        functools.partial(_peer_experts_kernel, n_blocks=nblk),
        grid=(n // t, nblk + 1),
        in_specs=[
            pl.BlockSpec((d, t), lambda i, e: (0, i)),
            score, score, score,
            pl.BlockSpec((nh, 8, t), lambda i, e: (0, 0, i)),
            pl.BlockSpec((eb, d), lambda i, e: (jnp.minimum(e, nblk - 1), 0)),
            pl.BlockSpec((eb, d), lambda i, e: (jnp.maximum(e - 1, 0), 0)),
        ],
        out_specs=pl.BlockSpec((t, d), lambda i, e: (i, 0)),
        out_shape=jax.ShapeDtypeStruct((n, d), F32),
        scratch_shapes=[
            pltpu.VMEM((nh, PEER_NKEYS, t), F32),
            pltpu.VMEM((eb, t), F32),
            pltpu.VMEM((eb, t), F32),
            pltpu.VMEM((2, eb, t), BF16),
        ],
        compiler_params=_cparams("parallel", "arbitrary"),
        name="peer_experts",
    )(ht, c, w, b, stats, expert_u, expert_v)


def _residual_rmsnorm_kernel(x_ref, y_ref, g_ref, o_ref):
    x = x_ref[...] + y_ref[...]
    ms = jnp.mean(x * x, axis=-1, keepdims=True)
    o_ref[...] = x * lax.rsqrt(ms + EPS) * g_ref[...]


def _residual_rmsnorm(x, y, gain, *, tm=512):
    n, d = x.shape
    row = pl.BlockSpec((tm, d), lambda i: (i, 0))
    return pl.pallas_call(
        _residual_rmsnorm_kernel,
        grid=(n // tm,),
        in_specs=[row, row, pl.BlockSpec((1, d), lambda i: (0, 0))],
        out_specs=row,
        out_shape=jax.ShapeDtypeStruct((n, d), F32),
        compiler_params=_cparams("parallel"),
        name="residual_rmsnorm",
    )(x, y, gain.reshape(1, d))


def _layer(x, mem, bsz, seq, p, final_gain):
    n, d = x.shape
    mlen = mem.shape[0] // bsz
    bf = lambda w: w.astype(BF16)
    w_in = p["w_in"]
    qkv_cols = 3 * ATT_WIDTH
    ubc_cols = qkv_cols + 3 * CONV_WIDTH

    qkv = _norm_matmul(x, p["norm_mix"], bf(w_in[:, :qkv_cols]), BF16)
    ubc = _norm_matmul(x, p["norm_mix"], bf(w_in[:, qkv_cols:ubc_cols]), F32)
    gates = _norm_matmul(x, p["norm_mix"], bf(w_in[:, ubc_cols:]), F32)

    att = _band_attention(qkv, _band_bias(p["rel_bias"]), bsz, seq)
    merged = _gated_merge(att, ubc, gates, p["conv_w"], p["conv_b"],
                          bf(p["w_att_out"]), bf(p["w_conv_out"]), seq)
    x = _matmul_residual(merged, bf(p["w_mix_out"]), x)

    q = _norm_matmul(x, p["norm_cross"], bf(p["w_cq"]), BF16)
    k = _norm_matmul(mem, p["norm_mem"], bf(p["w_ck"]), BF16)
    v = _norm_matmul(mem, p["norm_mem"], bf(p["w_cv"]), BF16)
    o = _cross_attention(q, k, v, bsz, seq, mlen)
    x = _matmul_residual(o, bf(p["w_co"]), x)

    pq, ht = _norm_matmul(x, p["norm_peer"], bf(p["w_pq"]), BF16, emit_h=True)
    cut, w, b, stats = _peer_select(pq, bf(p["sub_keys"]))
    y = _peer_experts(ht, cut, w, b, stats, bf(p["expert_u"]), bf(p["expert_v"]))
    return _residual_rmsnorm(x, y, final_gain)


def kernel(x, mem, norm_mix, w_in, conv_w, conv_b, rel_bias, w_att_out, w_conv_out, w_mix_out,
           norm_cross, norm_mem, w_cq, w_ck, w_cv, w_co, norm_peer, w_pq, sub_keys,
           expert_u, expert_v, norm_final):
    bsz, seq, d = x.shape
    depth = w_in.shape[0]
    assert depth == 1, "the fused final rmsnorm assumes a single layer"
    params = dict(norm_mix=norm_mix[0], w_in=w_in[0], conv_w=conv_w[0], conv_b=conv_b[0],
                  rel_bias=rel_bias[0], w_att_out=w_att_out[0], w_conv_out=w_conv_out[0],
                  w_mix_out=w_mix_out[0], norm_cross=norm_cross[0], norm_mem=norm_mem[0],
                  w_cq=w_cq[0], w_ck=w_ck[0], w_cv=w_cv[0], w_co=w_co[0], norm_peer=norm_peer[0],
                  w_pq=w_pq[0], sub_keys=sub_keys[0], expert_u=expert_u[0], expert_v=expert_v[0])
    out = _layer(x.reshape(bsz * seq, d), mem.reshape(-1, d), bsz, seq, params, norm_final)
    return out.reshape(bsz, seq, d)
```

```python
import functools
import math

import jax
import jax.numpy as jnp
from jax import lax
from jax.experimental import pallas as pl
from jax.experimental.pallas import tpu as pltpu

F32 = jnp.float32
BF16 = jnp.bfloat16

CHUNK = 64
LEFT_CHUNKS = 8
ATT_HEADS = 8
ATT_HEAD_DIM = 128
ATT_WIDTH = ATT_HEADS * ATT_HEAD_DIM
REL_CLIP = 128
CONV_WIDTH = 1024
MEM_HEADS = 4
PEER_HEADS = 8
PEER_NKEYS = 128
PEER_HALF = 128
PEER_TOPK = 16
EPS = 1e-6
NEG_INF = -1e30

LANES = 128
ATT_QBLOCK = 512
ATT_SUB = 256
ATT_KWIN = ATT_SUB + LEFT_CHUNKS * CHUNK
VMEM_LIMIT = 56 * 1024 * 1024


def _cparams(*sem, flags=None):
    return pltpu.CompilerParams(dimension_semantics=sem, vmem_limit_bytes=VMEM_LIMIT, flags=flags)


def _norm_matmul_kernel(x_ref, g_ref, w_ref, o_ref, *rest, emit_h):
    if emit_h:
        hout_ref, h_ref = rest
    else:
        (h_ref,) = rest

    @pl.when(pl.program_id(1) == 0)
    def _():
        x = x_ref[...]
        ms = jnp.mean(x * x, axis=-1, keepdims=True)
        h = (x * lax.rsqrt(ms + EPS) * g_ref[...]).astype(BF16)
        h_ref[...] = h
        if emit_h:
            hout_ref[...] = h.T

    o_ref[...] = jnp.dot(h_ref[...], w_ref[...], preferred_element_type=F32).astype(o_ref.dtype)


def _norm_matmul(x, gain, w, out_dtype, *, tm=1024, tn=1024, emit_h=False):
    n, d = x.shape
    c = w.shape[1]
    tm = min(tm, n)
    out_shape = [jax.ShapeDtypeStruct((n, c), out_dtype)]
    out_specs = [pl.BlockSpec((tm, tn), lambda i, j: (i, j))]
    if emit_h:
        out_shape.append(jax.ShapeDtypeStruct((d, n), BF16))
        out_specs.append(pl.BlockSpec((d, tm), lambda i, j: (0, i)))
    res = pl.pallas_call(
        functools.partial(_norm_matmul_kernel, emit_h=emit_h),
        grid=(n // tm, c // tn),
        in_specs=[
            pl.BlockSpec((tm, d), lambda i, j: (i, 0)),
            pl.BlockSpec((1, d), lambda i, j: (0, 0)),
            pl.BlockSpec((d, tn), lambda i, j: (0, j)),
        ],
        out_specs=out_specs,
        out_shape=out_shape,
        scratch_shapes=[pltpu.VMEM((tm, d), BF16)],
        compiler_params=_cparams("parallel", "arbitrary"),
        name="norm_matmul",
    )(x, gain.reshape(1, d), w)
    return res if emit_h else res[0]


def _matmul_residual_kernel(a_ref, w_ref, r_ref, o_ref):
    o_ref[...] = r_ref[...] + jnp.dot(a_ref[...], w_ref[...], preferred_element_type=F32)


def _matmul_residual(a, w, res, *, tm=1024, tn=1024):
    n, k = a.shape
    c = w.shape[1]
    return pl.pallas_call(
        _matmul_residual_kernel,
        grid=(n // tm, c // tn),
        in_specs=[
            pl.BlockSpec((tm, k), lambda i, j: (i, 0)),
            pl.BlockSpec((k, tn), lambda i, j: (0, j)),
            pl.BlockSpec((tm, tn), lambda i, j: (i, j)),
        ],
        out_specs=pl.BlockSpec((tm, tn), lambda i, j: (i, j)),
        out_shape=jax.ShapeDtypeStruct((n, c), F32),
        compiler_params=_cparams("parallel", "arbitrary"),
        name="matmul_residual",
    )(a, w, res)


def _band_attention_kernel(q_ref, kp_ref, kc_ref, vp_ref, vc_ref, bias_ref, o_ref, kwin_ref, vwin_ref):
    first = pl.program_id(1) == 0
    kwin_ref[0:ATT_QBLOCK, :] = kp_ref[...]
    kwin_ref[ATT_QBLOCK:, :] = kc_ref[...]
    vwin_ref[0:ATT_QBLOCK, :] = vp_ref[...]
    vwin_ref[ATT_QBLOCK:, :] = vc_ref[...]
    scale = ATT_HEAD_DIM ** -0.5
    col = lax.broadcasted_iota(jnp.int32, (ATT_SUB, ATT_KWIN), 1)
    for s in range(ATT_QBLOCK // ATT_SUB):
        r0 = s * ATT_SUB
        dead = jnp.logical_and(first, col < ATT_QBLOCK - r0)
        for h in range(ATT_HEADS):
            c0 = h * ATT_HEAD_DIM
            qh = q_ref[r0:r0 + ATT_SUB, c0:c0 + ATT_HEAD_DIM]
            kh = kwin_ref[r0:r0 + ATT_KWIN, c0:c0 + ATT_HEAD_DIM]
            vh = vwin_ref[r0:r0 + ATT_KWIN, c0:c0 + ATT_HEAD_DIM]
            sc = lax.dot_general(qh, kh, (((1,), (1,)), ((), ())), preferred_element_type=F32)
            sc = sc * scale + bias_ref[h]
            sc = jnp.where(dead, NEG_INF, sc)
            m = jnp.max(sc, axis=-1, keepdims=True)
            p = jnp.exp(sc - m)
            p = p / jnp.sum(p, axis=-1, keepdims=True)
            o = jnp.dot(p.astype(BF16), vh, preferred_element_type=F32)
            o_ref[r0:r0 + ATT_SUB, c0:c0 + ATT_HEAD_DIM] = o.astype(o_ref.dtype)


def _band_bias(rel_table):
    nh = rel_table.shape[0]
    r = jnp.arange(ATT_SUB)[:, None]
    c = jnp.arange(ATT_KWIN)[None, :]
    span = ATT_SUB + ATT_KWIN
    lead = ATT_SUB - 1 + LEFT_CHUNKS * CHUNK - REL_CLIP
    line = jnp.concatenate([
        jnp.broadcast_to(rel_table[:, -1:], (nh, lead)),
        rel_table[:, ::-1],
        jnp.broadcast_to(rel_table[:, :1], (nh, span - lead - 2 * REL_CLIP - 1)),
    ], axis=1)
    rows = jnp.tile(line, (1, ATT_SUB))[:, :ATT_SUB * (span - 1)].reshape(nh, ATT_SUB, span - 1)
    bias = rows[:, :, ATT_SUB - 1:ATT_SUB - 1 + ATT_KWIN]
    qc, kc = r // CHUNK, c // CHUNK
    in_band = jnp.logical_and(kc >= qc, kc <= qc + LEFT_CHUNKS)
    return jnp.where(in_band[None], bias, NEG_INF).astype(F32)


def _band_attention(qkv, bias, bsz, seq):
    n = bsz * seq
    nb = seq // ATT_QBLOCK
    blk = (ATT_QBLOCK, ATT_WIDTH)
    cur = lambda col: (lambda b, i: (b * nb + i, col))
    prev = lambda col: (lambda b, i: (b * nb + jnp.maximum(i - 1, 0), col))
    return pl.pallas_call(
        _band_attention_kernel,
        grid=(bsz, nb),
        in_specs=[
            pl.BlockSpec(blk, cur(0)),
            pl.BlockSpec(blk, prev(1)),
            pl.BlockSpec(blk, cur(1)),
            pl.BlockSpec(blk, prev(2)),
            pl.BlockSpec(blk, cur(2)),
            pl.BlockSpec((ATT_HEADS, ATT_SUB, ATT_KWIN), lambda b, i: (0, 0, 0)),
        ],
        out_specs=pl.BlockSpec(blk, lambda b, i: (b * nb + i, 0)),
        out_shape=jax.ShapeDtypeStruct((n, ATT_WIDTH), BF16),
        scratch_shapes=[pltpu.VMEM((2 * ATT_QBLOCK, ATT_WIDTH), BF16),
                        pltpu.VMEM((2 * ATT_QBLOCK, ATT_WIDTH), BF16)],
        compiler_params=_cparams("parallel", "arbitrary"),
        name="band_attention",
    )(qkv, qkv, qkv, qkv, qkv, bias)


def _gated_merge_kernel(att_ref, u_ref, b_ref, c_ref, uh_ref, ch_ref, cw_ref, cb_ref,
                        wa_ref, wc_ref, ga_ref, gb_ref, o_ref, yc_ref, *, tiles_per_seq):
    @pl.when(pl.program_id(1) == 0)
    def _():
        tm = u_ref.shape[0]
        z = c_ref[...] * u_ref[...]
        keep = (pl.program_id(0) % tiles_per_seq != 0).astype(F32)
        zh = ch_ref[...] * uh_ref[...] * keep
        row = lax.broadcasted_iota(jnp.int32, z.shape, 0)
        z1 = jnp.where(row == 0, zh[7:8], pltpu.roll(z, 1, 0))
        z2 = jnp.where(row == 0, zh[6:7], jnp.where(row == 1, zh[7:8], pltpu.roll(z, 2, 0)))
        y = z2 * cw_ref[0:1] + cb_ref[...]
        y = y + z1 * cw_ref[1:2]
        y = y + z * cw_ref[2:3]
        yc_ref[...] = (b_ref[...] * y).astype(BF16)

    y_att = jnp.dot(att_ref[...], wa_ref[...], preferred_element_type=F32)
    y_conv = jnp.dot(yc_ref[...], wc_ref[...], preferred_element_type=F32)
    merged = jax.nn.sigmoid(ga_ref[...]) * y_att + jax.nn.sigmoid(gb_ref[...]) * y_conv
    o_ref[...] = merged.astype(o_ref.dtype)


def _gated_merge(att, branch, conv_w, conv_b, w_att_out, w_conv_out, seq, *, tm=512):
    n = att.shape[0]
    d = w_att_out.shape[1]
    halo = 8
    hpt = tm // halo
    first_conv = 2 * d // CONV_WIDTH
    col = lambda cc: (lambda i, j: (i, cc))
    halo_map = lambda cc: (lambda i, j: (jnp.maximum(i * hpt - 1, 0), cc))
    return pl.pallas_call(
        functools.partial(_gated_merge_kernel, tiles_per_seq=seq // tm),
        grid=(n // tm, 1),
        in_specs=[
            pl.BlockSpec((tm, ATT_WIDTH), col(0)),
            pl.BlockSpec((tm, CONV_WIDTH), col(first_conv)),
            pl.BlockSpec((tm, CONV_WIDTH), col(first_conv + 1)),
            pl.BlockSpec((tm, CONV_WIDTH), col(first_conv + 2)),
            pl.BlockSpec((halo, CONV_WIDTH), halo_map(first_conv)),
            pl.BlockSpec((halo, CONV_WIDTH), halo_map(first_conv + 2)),
            pl.BlockSpec((3, CONV_WIDTH), lambda i, j: (0, 0)),
            pl.BlockSpec((1, CONV_WIDTH), lambda i, j: (0, 0)),
            pl.BlockSpec((ATT_WIDTH, d), lambda i, j: (0, 0)),
            pl.BlockSpec((CONV_WIDTH, d), lambda i, j: (0, 0)),
            pl.BlockSpec((tm, d), col(0)),
            pl.BlockSpec((tm, d), col(1)),
        ],
        out_specs=pl.BlockSpec((tm, d), col(0)),
        out_shape=jax.ShapeDtypeStruct((n, d), BF16),
        scratch_shapes=[pltpu.VMEM((tm, CONV_WIDTH), BF16)],
        compiler_params=_cparams("parallel", "arbitrary"),
        name="gated_merge",
    )(att, branch, branch, branch, branch, branch, conv_w, conv_b.reshape(1, CONV_WIDTH),
      w_att_out, w_conv_out, branch, branch)


def _cross_attention_kernel(q_ref, k_ref, v_ref, o_ref):
    dh = q_ref.shape[1] // MEM_HEADS
    scale = dh ** -0.5
    for h in range(MEM_HEADS):
        c0 = h * dh
        sc = lax.dot_general(q_ref[:, c0:c0 + dh], k_ref[:, c0:c0 + dh],
                             (((1,), (1,)), ((), ())), preferred_element_type=F32) * scale
        m = jnp.max(sc, axis=-1, keepdims=True)
        p = jnp.exp(sc - m)
        p = p / jnp.sum(p, axis=-1, keepdims=True)
        o = jnp.dot(p.astype(BF16), v_ref[:, c0:c0 + dh], preferred_element_type=F32)
        o_ref[:, c0:c0 + dh] = o.astype(o_ref.dtype)


def _cross_attention(q, k, v, bsz, seq, mlen, *, tm=512):
    n, d = q.shape
    nb = seq // tm
    return pl.pallas_call(
        _cross_attention_kernel,
        grid=(bsz, nb),
        in_specs=[
            pl.BlockSpec((tm, d), lambda b, i: (b * nb + i, 0)),
            pl.BlockSpec((mlen, d), lambda b, i: (b, 0)),
            pl.BlockSpec((mlen, d), lambda b, i: (b, 0)),
        ],
        out_specs=pl.BlockSpec((tm, d), lambda b, i: (b * nb + i, 0)),
        out_shape=jax.ShapeDtypeStruct((n, d), BF16),
        compiler_params=_cparams("parallel", "arbitrary"),
        name="cross_attention",
    )(q, k, v)


def _batcher_network(n):
    def merge(lo, hi, r):
        step = r * 2
        if step < hi - lo:
            yield from merge(lo, hi, step)
            yield from merge(lo + r, hi, step)
            yield from [(i, i + r) for i in range(lo + r, hi - r, step)]
        else:
            yield (lo, lo + r)

    def sort(lo, hi):
        if hi - lo >= 1:
            mid = lo + (hi - lo) // 2
            yield from sort(lo, mid)
            yield from sort(mid + 1, hi)
            yield from merge(lo, hi, 1)

    return list(sort(0, n - 1))


def _bitonic_network(n):
    pairs, d = [], n // 2
    while d >= 1:
        pairs += [(i, i + d) for i in range(n) if not i & d]
        d //= 2
    return pairs


def _apply_network(v, pairs):
    v = list(v)
    for i, j in pairs:
        v[i], v[j] = jnp.maximum(v[i], v[j]), jnp.minimum(v[i], v[j])
    return v


def _merge_sublanes(v, shift):
    n = len(v)
    w = [pltpu.roll(x, shift, 0) for x in v]
    return _apply_network([jnp.maximum(v[k], w[n - 1 - k]) for k in range(n)], _bitonic_network(n))


def _top16_over_keys(ref, h, lanes):
    v = [ref[h, 8 * k:8 * k + 8, lanes] for k in range(PEER_NKEYS // 8)]
    v = _apply_network(v, _batcher_network(len(v)))
    for shift in (4, 2, 1):
        v = _merge_sublanes(v, shift)
    return v


def _spread(v, sub):
    out = v[0]
    for k in range(1, 8):
        out = jnp.where(sub == k, v[k], out)
    return out


def _peer_select_kernel(q_ref, sk_ref, c_ref, w_ref, b_ref, st_ref, a_ref):
    t = q_ref.shape[0]
    for h in range(PEER_HEADS):
        c0 = h * 2 * PEER_HALF
        dn = (((1,), (1,)), ((), ()))
        a_ref[h] = lax.dot_general(sk_ref[h, 0], q_ref[:, c0:c0 + PEER_HALF], dn,
                                   preferred_element_type=F32)
        b_ref[h] = lax.dot_general(sk_ref[h, 1], q_ref[:, c0 + PEER_HALF:c0 + 2 * PEER_HALF], dn,
                                   preferred_element_type=F32)

    lane_tiles = t // LANES
    sub = lax.broadcasted_iota(jnp.int32, (8, LANES), 0)
    neg = -jnp.inf

    def body(it, carry):
        h = it // lane_tiles
        lanes = pl.ds(pl.multiple_of((it % lane_tiles) * LANES, LANES), LANES)
        a = _top16_over_keys(a_ref, h, lanes)
        b = _top16_over_keys(b_ref, h, lanes)
        a_lo, a_hi = _spread(a[:8], sub), _spread(a[8:], sub)
        b_lo, b_hi = _spread(b[:8], sub), _spread(b[8:], sub)
        cand = [
            a[0] + b_lo,
            a[0] + b_hi,
            a[1] + b_lo,
            jnp.where(sub >= 2, a_lo + b[0],
                      jnp.where(sub == 0, a[4] + b[2], neg)),
            a_hi + b[0],
            jnp.where(sub >= 2, a_lo + b[1], neg),
            jnp.where(sub >= 2, a[2] + b_lo, neg),
            jnp.where(sub >= 2, a[3] + b_lo, neg),
        ]
        cand = _apply_network(cand, _batcher_network(8))
        mirror = [pltpu.roll(x, 4, 0) for x in cand]
        best = _apply_network(cand + mirror[::-1], _bitonic_network(PEER_TOPK))
        for shift in (2, 1):
            best = _merge_sublanes(best, shift)
        z = jnp.zeros((8, LANES), F32)
        for v in best:
            z = z + jnp.exp(v - best[0])
        thr = best[PEER_TOPK - 1]
        log_norm = a[0] + jnp.log(z)
        st_ref[h, :, lanes] = b[0]
        for k in range(PEER_NKEYS // 8):
            rows = slice(8 * k, 8 * k + 8)
            a_k = a_ref[h, rows, lanes]
            cut = jnp.full((8, LANES), jnp.inf, F32)
            for r in range(PEER_TOPK):
                cut = jnp.where(a_k + b[r] >= thr, b[r], cut)
            c_ref[h, rows, lanes] = cut
            w_ref[h, rows, lanes] = 0.5 * jnp.exp(a_k - log_norm)
        return carry

    lax.fori_loop(0, PEER_HEADS * lane_tiles, body, 0)


def _peer_select(q, sub_keys, *, t=512):
    n = q.shape[0]
    nh = PEER_HEADS
    score = jax.ShapeDtypeStruct((nh, PEER_NKEYS, n), F32)
    score_spec = pl.BlockSpec((nh, PEER_NKEYS, t), lambda i: (0, 0, i))
    return pl.pallas_call(
        _peer_select_kernel,
        grid=(n // t,),
        in_specs=[
            pl.BlockSpec((t, q.shape[1]), lambda i: (i, 0)),
            pl.BlockSpec(sub_keys.shape, lambda i: (0, 0, 0, 0)),
        ],
        out_specs=[score_spec, score_spec, score_spec,
                   pl.BlockSpec((nh, 8, t), lambda i: (0, 0, i))],
        out_shape=[score, score, score, jax.ShapeDtypeStruct((nh, 8, n), F32)],
        scratch_shapes=[pltpu.VMEM((nh, PEER_NKEYS, t), F32)],
        compiler_params=_cparams("parallel"),
        name="peer_select",
    )(q, sub_keys)


def _peer_experts_kernel(ht_ref, c_ref, w_ref, b_ref, st_ref, u_ref, v_ref, o_ref,
                         eb_ref, hu_ref, gate_ref, act_ref, *, n_blocks):
    e = pl.program_id(1)
    t = ht_ref.shape[1]
    eb = u_ref.shape[0]
    assert eb == 8 * PEER_NKEYS
    apply_dn = (((0,), (0,)), ((), ()))

    @pl.when(e == 0)
    def _():
        o_ref[...] = jnp.zeros_like(o_ref)
        act_ref[1] = jnp.zeros((eb, t), BF16)
        for h in range(PEER_HEADS):
            eb_ref[h] = jnp.exp(b_ref[h] - st_ref[h, 0:1, :])

    @pl.when(e < n_blocks)
    def _():
        slot = e % 2
        hu_ref[...] = jnp.dot(u_ref[...], ht_ref[...], preferred_element_type=F32)
        i0 = pl.multiple_of(e * 8, 8)
        for lt in range(t // LANES):
            ls = slice(lt * LANES, (lt + 1) * LANES)
            c_rows = [c_ref[h, pl.ds(i0, 8), ls] for h in range(PEER_HEADS)]
            w_rows = [w_ref[h, pl.ds(i0, 8), ls] for h in range(PEER_HEADS)]
            for ii in range(8):
                gate = None
                for h in range(PEER_HEADS):
                    sel = b_ref[h, :, ls] >= c_rows[h][ii:ii + 1]
                    term = jnp.where(sel, eb_ref[h, :, ls], 0.0) * w_rows[h][ii:ii + 1]
                    gate = term if gate is None else gate + term
                gate_ref[ii * PEER_NKEYS:(ii + 1) * PEER_NKEYS, ls] = gate
        o_ref[...] += lax.dot_general(act_ref[1 - slot], v_ref[...], apply_dn,
                                      preferred_element_type=F32)
        hu = hu_ref[...]
        act = hu * (1.0 + lax.erf(hu * (2.0 ** -0.5)))
        act_ref[slot] = (act * gate_ref[...]).astype(BF16)

    @pl.when(e == n_blocks)
    def _():
        o_ref[...] += lax.dot_general(act_ref[(n_blocks - 1) % 2], v_ref[...], apply_dn,
                                      preferred_element_type=F32)


def _peer_experts(ht, c, w, b, stats, expert_u, expert_v, *, t=512, eb=8 * PEER_NKEYS):
    d, n = ht.shape
    nblk = expert_u.shape[0] // eb
    nh = PEER_HEADS
    score = pl.BlockSpec((nh, PEER_NKEYS, t), lambda i, e: (0, 0, i))
    return pl.pallas_call(
        functools.partial(_peer_experts_kernel, n_blocks=nblk),
        grid=(n // t, nblk + 1),
        in_specs=[
            pl.BlockSpec((d, t), lambda i, e: (0, i)),
            score, score, score,
            pl.BlockSpec((nh, 8, t), lambda i, e: (0, 0, i)),
            pl.BlockSpec((eb, d), lambda i, e: (jnp.minimum(e, nblk - 1), 0)),
            pl.BlockSpec((eb, d), lambda i, e: (jnp.maximum(e - 1, 0), 0)),
        ],
        out_specs=pl.BlockSpec((t, d), lambda i, e: (i, 0)),
        out_shape=jax.ShapeDtypeStruct((n, d), F32),
        scratch_shapes=[
            pltpu.VMEM((nh, PEER_NKEYS, t), F32),
            pltpu.VMEM((eb, t), F32),
            pltpu.VMEM((eb, t), F32),
            pltpu.VMEM((2, eb, t), BF16),
        ],
        compiler_params=_cparams("parallel", "arbitrary"),
        name="peer_experts",
    )(ht, c, w, b, stats, expert_u, expert_v)


def _residual_rmsnorm_kernel(x_ref, y_ref, g_ref, o_ref):
    x = x_ref[...] + y_ref[...]
    ms = jnp.mean(x * x, axis=-1, keepdims=True)
    o_ref[...] = x * lax.rsqrt(ms + EPS) * g_ref[...]


def _residual_rmsnorm(x, y, gain, *, tm=512):
    n, d = x.shape
    row = pl.BlockSpec((tm, d), lambda i: (i, 0))
    return pl.pallas_call(
        _residual_rmsnorm_kernel,
        grid=(n // tm,),
        in_specs=[row, row, pl.BlockSpec((1, d), lambda i: (0, 0))],
        out_specs=row,
        out_shape=jax.ShapeDtypeStruct((n, d), F32),
        compiler_params=_cparams("parallel"),
        name="residual_rmsnorm",
    )(x, y, gain.reshape(1, d))


def _layer(x, mem, bsz, seq, p, final_gain):
    n, d = x.shape
    mlen = mem.shape[0] // bsz
    bf = lambda w: w.astype(BF16)
    w_in = p["w_in"]
    qkv_cols = 3 * ATT_WIDTH
    ubc_cols = qkv_cols + 3 * CONV_WIDTH

    qkv = _norm_matmul(x, p["norm_mix"], bf(w_in[:, :qkv_cols]), BF16)
    w_branch = jnp.concatenate([w_in[:, ubc_cols:], w_in[:, qkv_cols:ubc_cols]], axis=1)
    branch = _norm_matmul(x, p["norm_mix"], bf(w_branch), F32)

    att = _band_attention(qkv, _band_bias(p["rel_bias"]), bsz, seq)
    merged = _gated_merge(att, branch, p["conv_w"], p["conv_b"],
                          bf(p["w_att_out"]), bf(p["w_conv_out"]), seq)
    x = _matmul_residual(merged, bf(p["w_mix_out"]), x)

    q = _norm_matmul(x, p["norm_cross"], bf(p["w_cq"]), BF16)
    k = _norm_matmul(mem, p["norm_mem"], bf(p["w_ck"]), BF16)
    v = _norm_matmul(mem, p["norm_mem"], bf(p["w_cv"]), BF16)
    o = _cross_attention(q, k, v, bsz, seq, mlen)
    x = _matmul_residual(o, bf(p["w_co"]), x)

    pq, ht = _norm_matmul(x, p["norm_peer"], bf(p["w_pq"]), BF16, emit_h=True)
    cut, w, b, stats = _peer_select(pq, bf(p["sub_keys"]))
    y = _peer_experts(ht, cut, w, b, stats, bf(p["expert_u"]), bf(p["expert_v"]))
    return _residual_rmsnorm(x, y, final_gain)


def kernel(x, mem, norm_mix, w_in, conv_w, conv_b, rel_bias, w_att_out, w_conv_out, w_mix_out,
           norm_cross, norm_mem, w_cq, w_ck, w_cv, w_co, norm_peer, w_pq, sub_keys,
           expert_u, expert_v, norm_final):
    bsz, seq, d = x.shape
    depth = w_in.shape[0]
    assert depth == 1, "the fused final rmsnorm assumes a single layer"
    params = dict(norm_mix=norm_mix[0], w_in=w_in[0], conv_w=conv_w[0], conv_b=conv_b[0],
                  rel_bias=rel_bias[0], w_att_out=w_att_out[0], w_conv_out=w_conv_out[0],
                  w_mix_out=w_mix_out[0], norm_cross=norm_cross[0], norm_mem=norm_mem[0],
                  w_cq=w_cq[0], w_ck=w_ck[0], w_cv=w_cv[0], w_co=w_co[0], norm_peer=norm_peer[0],
                  w_pq=w_pq[0], sub_keys=sub_keys[0], expert_u=expert_u[0], expert_v=expert_v[0])
    out = _layer(x.reshape(bsz * seq, d), mem.reshape(-1, d), bsz, seq, params, norm_final)
    return out.reshape(bsz, seq, d)
```

```python
import functools
import math

import jax
import jax.numpy as jnp
from jax import lax
from jax.experimental import pallas as pl
from jax.experimental.pallas import tpu as pltpu

F32 = jnp.float32
BF16 = jnp.bfloat16

CHUNK = 64
LEFT_CHUNKS = 8
ATT_HEADS = 8
ATT_HEAD_DIM = 128
ATT_WIDTH = ATT_HEADS * ATT_HEAD_DIM
REL_CLIP = 128
CONV_WIDTH = 1024
MEM_HEADS = 4
PEER_HEADS = 8
PEER_NKEYS = 128
PEER_HALF = 128
PEER_TOPK = 16
EPS = 1e-6
NEG_INF = -1e30

LANES = 128
ATT_QBLOCK = 512
ATT_SUB = 256
ATT_KWIN = ATT_SUB + LEFT_CHUNKS * CHUNK
VMEM_LIMIT = 56 * 1024 * 1024


def _cparams(*sem, flags=None):
    return pltpu.CompilerParams(dimension_semantics=sem, vmem_limit_bytes=VMEM_LIMIT, flags=flags)


def _norm_matmul_kernel(x_ref, g_ref, w_ref, o_ref, *rest, emit_h):
    if emit_h:
        hout_ref, h_ref = rest
    else:
        (h_ref,) = rest

    @pl.when(pl.program_id(1) == 0)
    def _():
        x = x_ref[...]
        ms = jnp.mean(x * x, axis=-1, keepdims=True)
        h = (x * lax.rsqrt(ms + EPS) * g_ref[...]).astype(BF16)
        h_ref[...] = h
        if emit_h:
            hout_ref[...] = h.T

    o_ref[...] = jnp.dot(h_ref[...], w_ref[...], preferred_element_type=F32).astype(o_ref.dtype)


def _norm_matmul(x, gain, w, out_dtype, *, tm=1024, tn=1024, emit_h=False):
    n, d = x.shape
    c = w.shape[1]
    tm = min(tm, n)
    out_shape = [jax.ShapeDtypeStruct((n, c), out_dtype)]
    out_specs = [pl.BlockSpec((tm, tn), lambda i, j: (i, j))]
    if emit_h:
        out_shape.append(jax.ShapeDtypeStruct((d, n), BF16))
        out_specs.append(pl.BlockSpec((d, tm), lambda i, j: (0, i)))
    res = pl.pallas_call(
        functools.partial(_norm_matmul_kernel, emit_h=emit_h),
        grid=(n // tm, c // tn),
        in_specs=[
            pl.BlockSpec((tm, d), lambda i, j: (i, 0)),
            pl.BlockSpec((1, d), lambda i, j: (0, 0)),
            pl.BlockSpec((d, tn), lambda i, j: (0, j)),
        ],
        out_specs=out_specs,
        out_shape=out_shape,
        scratch_shapes=[pltpu.VMEM((tm, d), BF16)],
        compiler_params=_cparams("parallel", "arbitrary"),
        name="norm_matmul",
    )(x, gain.reshape(1, d), w)
    return res if emit_h else res[0]


def _matmul_residual_kernel(a_ref, w_ref, r_ref, o_ref):
    o_ref[...] = r_ref[...] + jnp.dot(a_ref[...], w_ref[...], preferred_element_type=F32)


def _matmul_residual(a, w, res, *, tm=512, tn=2048):
    n, k = a.shape
    c = w.shape[1]
    return pl.pallas_call(
        _matmul_residual_kernel,
        grid=(n // tm, c // tn),
        in_specs=[
            pl.BlockSpec((tm, k), lambda i, j: (i, 0)),
            pl.BlockSpec((k, tn), lambda i, j: (0, j)),
            pl.BlockSpec((tm, tn), lambda i, j: (i, j)),
        ],
        out_specs=pl.BlockSpec((tm, tn), lambda i, j: (i, j)),
        out_shape=jax.ShapeDtypeStruct((n, c), F32),
        compiler_params=_cparams("parallel", "arbitrary"),
        name="matmul_residual",
    )(a, w, res)


def _band_attention_kernel(q_ref, kp_ref, kc_ref, vp_ref, vc_ref, bias_ref, o_ref, kwin_ref, vwin_ref):
    first = pl.program_id(1) == 0
    kwin_ref[0:ATT_QBLOCK, :] = kp_ref[...]
    kwin_ref[ATT_QBLOCK:, :] = kc_ref[...]
    vwin_ref[0:ATT_QBLOCK, :] = vp_ref[...]
    vwin_ref[ATT_QBLOCK:, :] = vc_ref[...]
    scale = ATT_HEAD_DIM ** -0.5
    col = lax.broadcasted_iota(jnp.int32, (ATT_SUB, ATT_KWIN), 1)
    for s in range(ATT_QBLOCK // ATT_SUB):
        r0 = s * ATT_SUB
        dead = jnp.logical_and(first, col < ATT_QBLOCK - r0)
        for h in range(ATT_HEADS):
            c0 = h * ATT_HEAD_DIM
            qh = q_ref[r0:r0 + ATT_SUB, c0:c0 + ATT_HEAD_DIM]
            kh = kwin_ref[r0:r0 + ATT_KWIN, c0:c0 + ATT_HEAD_DIM]
            vh = vwin_ref[r0:r0 + ATT_KWIN, c0:c0 + ATT_HEAD_DIM]
            sc = lax.dot_general(qh, kh, (((1,), (1,)), ((), ())), preferred_element_type=F32)
            sc = sc * scale + bias_ref[h]
            sc = jnp.where(dead, NEG_INF, sc)
            m = jnp.max(sc, axis=-1, keepdims=True)
            p = jnp.exp(sc - m)
            p = p / jnp.sum(p, axis=-1, keepdims=True)
            o = jnp.dot(p.astype(BF16), vh, preferred_element_type=F32)
            o_ref[r0:r0 + ATT_SUB, c0:c0 + ATT_HEAD_DIM] = o.astype(o_ref.dtype)


def _band_bias(rel_table):
    nh = rel_table.shape[0]
    r = jnp.arange(ATT_SUB)[:, None]
    c = jnp.arange(ATT_KWIN)[None, :]
    span = ATT_SUB + ATT_KWIN
    lead = ATT_SUB - 1 + LEFT_CHUNKS * CHUNK - REL_CLIP
    line = jnp.concatenate([
        jnp.broadcast_to(rel_table[:, -1:], (nh, lead)),
        rel_table[:, ::-1],
        jnp.broadcast_to(rel_table[:, :1], (nh, span - lead - 2 * REL_CLIP - 1)),
    ], axis=1)
    rows = jnp.tile(line, (1, ATT_SUB))[:, :ATT_SUB * (span - 1)].reshape(nh, ATT_SUB, span - 1)
    bias = rows[:, :, ATT_SUB - 1:ATT_SUB - 1 + ATT_KWIN]
    qc, kc = r // CHUNK, c // CHUNK
    in_band = jnp.logical_and(kc >= qc, kc <= qc + LEFT_CHUNKS)
    return jnp.where(in_band[None], bias, NEG_INF).astype(F32)


def _band_attention(qkv, bias, bsz, seq):
    n = bsz * seq
    nb = seq // ATT_QBLOCK
    blk = (ATT_QBLOCK, ATT_WIDTH)
    cur = lambda col: (lambda b, i: (b * nb + i, col))
    prev = lambda col: (lambda b, i: (b * nb + jnp.maximum(i - 1, 0), col))
    return pl.pallas_call(
        _band_attention_kernel,
        grid=(bsz, nb),
        in_specs=[
            pl.BlockSpec(blk, cur(0)),
            pl.BlockSpec(blk, prev(1)),
            pl.BlockSpec(blk, cur(1)),
            pl.BlockSpec(blk, prev(2)),
            pl.BlockSpec(blk, cur(2)),
            pl.BlockSpec((ATT_HEADS, ATT_SUB, ATT_KWIN), lambda b, i: (0, 0, 0)),
        ],
        out_specs=pl.BlockSpec(blk, lambda b, i: (b * nb + i, 0)),
        out_shape=jax.ShapeDtypeStruct((n, ATT_WIDTH), BF16),
        scratch_shapes=[pltpu.VMEM((2 * ATT_QBLOCK, ATT_WIDTH), BF16),
                        pltpu.VMEM((2 * ATT_QBLOCK, ATT_WIDTH), BF16)],
        compiler_params=_cparams("parallel", "arbitrary"),
        name="band_attention",
    )(qkv, qkv, qkv, qkv, qkv, bias)


def _gated_merge_kernel(att_ref, u_ref, b_ref, c_ref, uh_ref, ch_ref, cw_ref, cb_ref,
                        wa_ref, wc_ref, ga_ref, gb_ref, o_ref, yc_ref, *, tiles_per_seq):
    @pl.when(pl.program_id(1) == 0)
    def _():
        tm = u_ref.shape[0]
        z = c_ref[...] * u_ref[...]
        keep = (pl.program_id(0) % tiles_per_seq != 0).astype(F32)
        zh = ch_ref[...] * uh_ref[...] * keep
        row = lax.broadcasted_iota(jnp.int32, z.shape, 0)
        z1 = jnp.where(row == 0, zh[7:8], pltpu.roll(z, 1, 0))
        z2 = jnp.where(row == 0, zh[6:7], jnp.where(row == 1, zh[7:8], pltpu.roll(z, 2, 0)))
        y = z2 * cw_ref[0:1] + cb_ref[...]
        y = y + z1 * cw_ref[1:2]
        y = y + z * cw_ref[2:3]
        yc_ref[...] = (b_ref[...] * y).astype(BF16)

    y_att = jnp.dot(att_ref[...], wa_ref[...], preferred_element_type=F32)
    y_conv = jnp.dot(yc_ref[...], wc_ref[...], preferred_element_type=F32)
    merged = jax.nn.sigmoid(ga_ref[...]) * y_att + jax.nn.sigmoid(gb_ref[...]) * y_conv
    o_ref[...] = merged.astype(o_ref.dtype)


def _gated_merge(att, branch, conv_w, conv_b, w_att_out, w_conv_out, seq, *, tm=512):
    n = att.shape[0]
    d = w_att_out.shape[1]
    halo = 8
    hpt = tm // halo
    first_conv = 2 * d // CONV_WIDTH
    col = lambda cc: (lambda i, j: (i, cc))
    halo_map = lambda cc: (lambda i, j: (jnp.maximum(i * hpt - 1, 0), cc))
    return pl.pallas_call(
        functools.partial(_gated_merge_kernel, tiles_per_seq=seq // tm),
        grid=(n // tm, 1),
        in_specs=[
            pl.BlockSpec((tm, ATT_WIDTH), col(0)),
            pl.BlockSpec((tm, CONV_WIDTH), col(first_conv)),
            pl.BlockSpec((tm, CONV_WIDTH), col(first_conv + 1)),
            pl.BlockSpec((tm, CONV_WIDTH), col(first_conv + 2)),
            pl.BlockSpec((halo, CONV_WIDTH), halo_map(first_conv)),
            pl.BlockSpec((halo, CONV_WIDTH), halo_map(first_conv + 2)),
            pl.BlockSpec((3, CONV_WIDTH), lambda i, j: (0, 0)),
            pl.BlockSpec((1, CONV_WIDTH), lambda i, j: (0, 0)),
            pl.BlockSpec((ATT_WIDTH, d), lambda i, j: (0, 0)),
            pl.BlockSpec((CONV_WIDTH, d), lambda i, j: (0, 0)),
            pl.BlockSpec((tm, d), col(0)),
            pl.BlockSpec((tm, d), col(1)),
        ],
        out_specs=pl.BlockSpec((tm, d), col(0)),
        out_shape=jax.ShapeDtypeStruct((n, d), BF16),
        scratch_shapes=[pltpu.VMEM((tm, CONV_WIDTH), BF16)],
        compiler_params=_cparams("parallel", "arbitrary"),
        name="gated_merge",
    )(att, branch, branch, branch, branch, branch, conv_w, conv_b.reshape(1, CONV_WIDTH),
      w_att_out, w_conv_out, branch, branch)


def _cross_attention_kernel(q_ref, k_ref, v_ref, o_ref):
    dh = q_ref.shape[1] // MEM_HEADS
    scale = dh ** -0.5
    for h in range(MEM_HEADS):
        c0 = h * dh
        sc = lax.dot_general(q_ref[:, c0:c0 + dh], k_ref[:, c0:c0 + dh],
                             (((1,), (1,)), ((), ())), preferred_element_type=F32) * scale
        m = jnp.max(sc, axis=-1, keepdims=True)
        p = jnp.exp(sc - m)
        p = p / jnp.sum(p, axis=-1, keepdims=True)
        o = jnp.dot(p.astype(BF16), v_ref[:, c0:c0 + dh], preferred_element_type=F32)
        o_ref[:, c0:c0 + dh] = o.astype(o_ref.dtype)


def _cross_attention(q, k, v, bsz, seq, mlen, *, tm=512):
    n, d = q.shape
    nb = seq // tm
    return pl.pallas_call(
        _cross_attention_kernel,
        grid=(bsz, nb),
        in_specs=[
            pl.BlockSpec((tm, d), lambda b, i: (b * nb + i, 0)),
            pl.BlockSpec((mlen, d), lambda b, i: (b, 0)),
            pl.BlockSpec((mlen, d), lambda b, i: (b, 0)),
        ],
        out_specs=pl.BlockSpec((tm, d), lambda b, i: (b * nb + i, 0)),
        out_shape=jax.ShapeDtypeStruct((n, d), BF16),
        compiler_params=_cparams("parallel", "arbitrary"),
        name="cross_attention",
    )(q, k, v)


def _batcher_network(n):
    def merge(lo, hi, r):
        step = r * 2
        if step < hi - lo:
            yield from merge(lo, hi, step)
            yield from merge(lo + r, hi, step)
            yield from [(i, i + r) for i in range(lo + r, hi - r, step)]
        else:
            yield (lo, lo + r)

    def sort(lo, hi):
        if hi - lo >= 1:
            mid = lo + (hi - lo) // 2
            yield from sort(lo, mid)
            yield from sort(mid + 1, hi)
            yield from merge(lo, hi, 1)

    return list(sort(0, n - 1))


def _bitonic_network(n):
    pairs, d = [], n // 2
    while d >= 1:
        pairs += [(i, i + d) for i in range(n) if not i & d]
        d //= 2
    return pairs


def _apply_network(v, pairs):
    v = list(v)
    for i, j in pairs:
        v[i], v[j] = jnp.maximum(v[i], v[j]), jnp.minimum(v[i], v[j])
    return v


def _merge_sublanes(v, shift):
    n = len(v)
    w = [pltpu.roll(x, shift, 0) for x in v]
    return _apply_network([jnp.maximum(v[k], w[n - 1 - k]) for k in range(n)], _bitonic_network(n))


def _top16_over_keys(ref, h, lanes):
    v = [ref[h, 8 * k:8 * k + 8, lanes] for k in range(PEER_NKEYS // 8)]
    v = _apply_network(v, _batcher_network(len(v)))
    for shift in (4, 2, 1):
        v = _merge_sublanes(v, shift)
    return v


def _spread(v, sub):
    out = v[0]
    for k in range(1, 8):
        out = jnp.where(sub == k, v[k], out)
    return out


def _peer_select_kernel(q_ref, sk_ref, c_ref, w_ref, b_ref, st_ref, a_ref):
    t = q_ref.shape[0]
    for h in range(PEER_HEADS):
        c0 = h * 2 * PEER_HALF
        dn = (((1,), (1,)), ((), ()))
        a_ref[h] = lax.dot_general(sk_ref[h, 0], q_ref[:, c0:c0 + PEER_HALF], dn,
                                   preferred_element_type=F32)
        b_ref[h] = lax.dot_general(sk_ref[h, 1], q_ref[:, c0 + PEER_HALF:c0 + 2 * PEER_HALF], dn,
                                   preferred_element_type=F32)

    lane_tiles = t // LANES
    sub = lax.broadcasted_iota(jnp.int32, (8, LANES), 0)
    neg = -jnp.inf

    def body(it, carry):
        h = it // lane_tiles
        lanes = pl.ds(pl.multiple_of((it % lane_tiles) * LANES, LANES), LANES)
        a = _top16_over_keys(a_ref, h, lanes)
        b = _top16_over_keys(b_ref, h, lanes)
        a_lo, a_hi = _spread(a[:8], sub), _spread(a[8:], sub)
        b_lo, b_hi = _spread(b[:8], sub), _spread(b[8:], sub)
        cand = [
            a[0] + b_lo,
            a[0] + b_hi,
            a[1] + b_lo,
            jnp.where(sub >= 2, a_lo + b[0],
                      jnp.where(sub == 0, a[4] + b[2], neg)),
            a_hi + b[0],
            jnp.where(sub >= 2, a_lo + b[1], neg),
            jnp.where(sub >= 2, a[2] + b_lo, neg),
            jnp.where(sub >= 2, a[3] + b_lo, neg),
        ]
        cand = _apply_network(cand, _batcher_network(8))
        mirror = [pltpu.roll(x, 4, 0) for x in cand]
        best = _apply_network(cand + mirror[::-1], _bitonic_network(PEER_TOPK))
        for shift in (2, 1):
            best = _merge_sublanes(best, shift)
        z = jnp.zeros((8, LANES), F32)
        for v in best:
            z = z + jnp.exp(v - best[0])
        thr = best[PEER_TOPK - 1]
        log_norm = a[0] + jnp.log(z)
        st_ref[h, :, lanes] = b[0]
        for k in range(PEER_NKEYS // 8):
            rows = slice(8 * k, 8 * k + 8)
            a_k = a_ref[h, rows, lanes]
            cut = jnp.full((8, LANES), jnp.inf, F32)
            for r in range(PEER_TOPK):
                cut = jnp.where(a_k + b[r] >= thr, b[r], cut)
            c_ref[h, rows, lanes] = cut
            w_ref[h, rows, lanes] = 0.5 * jnp.exp(a_k - log_norm)
        return carry

    lax.fori_loop(0, PEER_HEADS * lane_tiles, body, 0)


def _peer_select(q, sub_keys, *, t=512):
    n = q.shape[0]
    nh = PEER_HEADS
    score = jax.ShapeDtypeStruct((nh, PEER_NKEYS, n), F32)
    score_spec = pl.BlockSpec((nh, PEER_NKEYS, t), lambda i: (0, 0, i))
    return pl.pallas_call(
        _peer_select_kernel,
        grid=(n // t,),
        in_specs=[
            pl.BlockSpec((t, q.shape[1]), lambda i: (i, 0)),
            pl.BlockSpec(sub_keys.shape, lambda i: (0, 0, 0, 0)),
        ],
        out_specs=[score_spec, score_spec, score_spec,
                   pl.BlockSpec((nh, 8, t), lambda i: (0, 0, i))],
        out_shape=[score, score, score, jax.ShapeDtypeStruct((nh, 8, n), F32)],
        scratch_shapes=[pltpu.VMEM((nh, PEER_NKEYS, t), F32)],
        compiler_params=_cparams("parallel"),
        name="peer_select",
    )(q, sub_keys)


def _peer_experts_kernel(ht_ref, c_ref, w_ref, b_ref, st_ref, u_ref, v_ref, o_ref,
                         eb_ref, hu_ref, gate_ref, act_ref, *, n_blocks):
    e = pl.program_id(1)
    t = ht_ref.shape[1]
    eb = u_ref.shape[0]
    assert eb == 8 * PEER_NKEYS
    apply_dn = (((0,), (0,)), ((), ()))

    @pl.when(e == 0)
    def _():
        o_ref[...] = jnp.zeros_like(o_ref)
        act_ref[1] = jnp.zeros((eb, t), BF16)
        for h in range(PEER_HEADS):
            eb_ref[h] = jnp.exp(b_ref[h] - st_ref[h, 0:1, :])

    @pl.when(e < n_blocks)
    def _():
        slot = e % 2
        hu_ref[...] = jnp.dot(u_ref[...], ht_ref[...], preferred_element_type=F32)
        i0 = pl.multiple_of(e * 8, 8)
        for lt in range(t // LANES):
            ls = slice(lt * LANES, (lt + 1) * LANES)
            c_rows = [c_ref[h, pl.ds(i0, 8), ls] for h in range(PEER_HEADS)]
            w_rows = [w_ref[h, pl.ds(i0, 8), ls] for h in range(PEER_HEADS)]
            for ii in range(8):
                gate = None
                for h in range(PEER_HEADS):
                    sel = b_ref[h, :, ls] >= c_rows[h][ii:ii + 1]
                    term = jnp.where(sel, eb_ref[h, :, ls], 0.0) * w_rows[h][ii:ii + 1]
                    gate = term if gate is None else gate + term
                gate_ref[ii * PEER_NKEYS:(ii + 1) * PEER_NKEYS, ls] = gate
        o_ref[...] += lax.dot_general(act_ref[1 - slot], v_ref[...], apply_dn,
                                      preferred_element_type=F32)
        hu = hu_ref[...]
        act = hu * (1.0 + lax.erf(hu * (2.0 ** -0.5)))
        act_ref[slot] = (act * gate_ref[...]).astype(BF16)

    @pl.when(e == n_blocks)
    def _():
        o_ref[...] += lax.dot_general(act_ref[(n_blocks - 1) % 2], v_ref[...], apply_dn,
                                      preferred_element_type=F32)


def _peer_experts(ht, c, w, b, stats, expert_u, expert_v, *, t=512, eb=8 * PEER_NKEYS):
    d, n = ht.shape
    nblk = expert_u.shape[0] // eb
    nh = PEER_HEADS
    score = pl.BlockSpec((nh, PEER_NKEYS, t), lambda i, e: (0, 0, i))
    return pl.pallas_call(
        functools.partial(_peer_experts_kernel, n_blocks=nblk),
        grid=(n // t, nblk + 1),
        in_specs=[
            pl.BlockSpec((d, t), lambda i, e: (0, i)),
            score, score, score,
            pl.BlockSpec((nh, 8, t), lambda i, e: (0, 0, i)),
            pl.BlockSpec((eb, d), lambda i, e: (jnp.minimum(e, nblk - 1), 0)),
            pl.BlockSpec((eb, d), lambda i, e: (jnp.maximum(e - 1, 0), 0)),
        ],
        out_specs=pl.BlockSpec((t, d), lambda i, e: (i, 0)),
        out_shape=jax.ShapeDtypeStruct((n, d), F32),
        scratch_shapes=[
            pltpu.VMEM((nh, PEER_NKEYS, t), F32),
            pltpu.VMEM((eb, t), F32),
            pltpu.VMEM((eb, t), F32),
            pltpu.VMEM((2, eb, t), BF16),
        ],
        compiler_params=_cparams("parallel", "arbitrary"),
        name="peer_experts",
    )(ht, c, w, b, stats, expert_u, expert_v)


def _residual_rmsnorm_kernel(x_ref, y_ref, g_ref, o_ref):
    x = x_ref[...] + y_ref[...]
    ms = jnp.mean(x * x, axis=-1, keepdims=True)
    o_ref[...] = x * lax.rsqrt(ms + EPS) * g_ref[...]


def _residual_rmsnorm(x, y, gain, *, tm=512):
    n, d = x.shape
    row = pl.BlockSpec((tm, d), lambda i: (i, 0))
    return pl.pallas_call(
        _residual_rmsnorm_kernel,
        grid=(n // tm,),
        in_specs=[row, row, pl.BlockSpec((1, d), lambda i: (0, 0))],
        out_specs=row,
        out_shape=jax.ShapeDtypeStruct((n, d), F32),
        compiler_params=_cparams("parallel"),
        name="residual_rmsnorm",
    )(x, y, gain.reshape(1, d))


def _layer(x, mem, bsz, seq, p, final_gain):
    n, d = x.shape
    mlen = mem.shape[0] // bsz
    bf = lambda w: w.astype(BF16)
    w_in = p["w_in"]
    qkv_cols = 3 * ATT_WIDTH
    ubc_cols = qkv_cols + 3 * CONV_WIDTH

    qkv = _norm_matmul(x, p["norm_mix"], bf(w_in[:, :qkv_cols]), BF16)
    w_branch = jnp.concatenate([w_in[:, ubc_cols:], w_in[:, qkv_cols:ubc_cols]], axis=1)
    branch = _norm_matmul(x, p["norm_mix"], bf(w_branch), F32)

    att = _band_attention(qkv, _band_bias(p["rel_bias"]), bsz, seq)
    merged = _gated_merge(att, branch, p["conv_w"], p["conv_b"],
                          bf(p["w_att_out"]), bf(p["w_conv_out"]), seq)
    x = _matmul_residual(merged, bf(p["w_mix_out"]), x)

    q = _norm_matmul(x, p["norm_cross"], bf(p["w_cq"]), BF16)
    k = _norm_matmul(mem, p["norm_mem"], bf(p["w_ck"]), BF16)
    v = _norm_matmul(mem, p["norm_mem"], bf(p["w_cv"]), BF16)
    o = _cross_attention(q, k, v, bsz, seq, mlen)
    x = _matmul_residual(o, bf(p["w_co"]), x)

    pq, ht = _norm_matmul(x, p["norm_peer"], bf(p["w_pq"]), BF16, emit_h=True)
    cut, w, b, stats = _peer_select(pq, bf(p["sub_keys"]))
    y = _peer_experts(ht, cut, w, b, stats, bf(p["expert_u"]), bf(p["expert_v"]))
    return _residual_rmsnorm(x, y, final_gain)


def kernel(x, mem, norm_mix, w_in, conv_w, conv_b, rel_bias, w_att_out, w_conv_out, w_mix_out,
           norm_cross, norm_mem, w_cq, w_ck, w_cv, w_co, norm_peer, w_pq, sub_keys,
           expert_u, expert_v, norm_final):
    bsz, seq, d = x.shape
    depth = w_in.shape[0]
    assert depth == 1, "the fused final rmsnorm assumes a single layer"
    params = dict(norm_mix=norm_mix[0], w_in=w_in[0], conv_w=conv_w[0], conv_b=conv_b[0],
                  rel_bias=rel_bias[0], w_att_out=w_att_out[0], w_conv_out=w_conv_out[0],
                  w_mix_out=w_mix_out[0], norm_cross=norm_cross[0], norm_mem=norm_mem[0],
                  w_cq=w_cq[0], w_ck=w_ck[0], w_cv=w_cv[0], w_co=w_co[0], norm_peer=norm_peer[0],
                  w_pq=w_pq[0], sub_keys=sub_keys[0], expert_u=expert_u[0], expert_v=expert_v[0])
    out = _layer(x.reshape(bsz * seq, d), mem.reshape(-1, d), bsz, seq, params, norm_final)
    return out.reshape(bsz, seq, d)
```
